```python
import jax, jax.numpy as jnp
from jax import lax
import numpy as np

D_MODEL = 1024
BATCH = 8
SEQ = 8192
DEPTH = 2

HEAD_DIM = 64
SWA_Q_HEADS = 8
SWA_KV_HEADS = 2
SWA_WINDOW = 128
FOX_HEADS = 8
BLOCK = 128
D_FF = 2816
CONV_WIDTH = 3
LN_EPS = 1e-5
NEG_INF = -1e30
FOX_GATE_BIAS_INIT = 3.0
DEEPNORM_ALPHA = (2 * DEPTH) ** 0.25
DEEPNORM_BETA = (8 * DEPTH) ** -0.25
SWA_Q = SWA_Q_HEADS * HEAD_DIM
SWA_KV = SWA_KV_HEADS * HEAD_DIM
FOX_W = FOX_HEADS * HEAD_DIM
SPLIT_SIZES = (SWA_Q, SWA_KV, SWA_KV, FOX_W, FOX_W, FOX_W, FOX_HEADS, D_MODEL, D_MODEL)
N_IN = SWA_Q + 2 * SWA_KV + 3 * FOX_W + FOX_HEADS + 2 * D_MODEL

kernel_name = "hybrid_swa_sink_fox_gated_convffn_deepnorm"


def layer_norm(x, g, b):
    xf = x.astype(jnp.float32)
    mu = jnp.mean(xf, axis=-1, keepdims=True)
    var = jnp.mean(jnp.square(xf - mu), axis=-1, keepdims=True)
    y = (xf - mu) * lax.rsqrt(var + LN_EPS) * g.astype(jnp.float32) + b.astype(jnp.float32)
    return y.astype(x.dtype)


def alibi_slopes(n_heads):
    return jnp.asarray(2.0 ** (-8.0 * np.arange(1, n_heads + 1) / n_heads), dtype=jnp.float32)


def swa_sink_attention(q, k, v, sinks):
    B, S, Hq, d = q.shape
    Hkv = k.shape[2]
    G = Hq // Hkv
    nb = S // BLOCK
    qb = q.reshape(B, nb, BLOCK, Hkv, G, d)
    kb = k.reshape(B, nb, BLOCK, Hkv, d)
    vb = v.reshape(B, nb, BLOCK, Hkv, d)

    def with_prev(t):
        prev = jnp.pad(t, ((0, 0), (1, 0), (0, 0), (0, 0), (0, 0)))[:, :-1]
        return jnp.concatenate([prev, t], axis=2)

    kw, vw = with_prev(kb), with_prev(vb)
    scores = jnp.einsum('bnqhgd,bnkhd->bnhgqk', qb, kw).astype(jnp.float32) * (d ** -0.5)
    q_pos = jnp.arange(BLOCK)[:, None] + BLOCK
    k_pos = jnp.arange(2 * BLOCK)[None, :]
    dist = q_pos - k_pos
    valid = (dist >= 0) & (dist < SWA_WINDOW)
    blk = jnp.arange(nb)[:, None, None]
    valid = valid[None] & ((k_pos[None] >= BLOCK) | (blk > 0))
    slopes = alibi_slopes(Hq).reshape(Hkv, G)
    alibi = -slopes[:, :, None, None] * dist.astype(jnp.float32)[None, None]
    scores = scores + alibi[None, None]
    scores = jnp.where(valid[None, :, None, None], scores, NEG_INF)
    sink = jnp.broadcast_to(sinks.astype(jnp.float32).reshape(Hkv, G)[None, None, :, :, None, None],
                            scores.shape[:-1] + (1,))
    probs = jax.nn.softmax(jnp.concatenate([scores, sink], axis=-1), axis=-1)[..., :-1]
    out = jnp.einsum('bnhgqk,bnkhd->bnqhgd', probs.astype(v.dtype), vw)
    return out.reshape(B, S, Hq * d)


def forgetting_attention(q, k, v, log_f):
    B, S, H, d = q.shape
    nb = S // BLOCK
    c = jnp.cumsum(log_f, axis=1)
    c_k = jnp.transpose(c, (0, 2, 1))
    qb = jnp.transpose(q.reshape(B, nb, BLOCK, H, d), (1, 0, 2, 3, 4))
    cq = jnp.transpose(c.reshape(B, nb, BLOCK, H), (1, 0, 3, 2))
    k_pos = jnp.arange(S)
    scale = d ** -0.5

    def one_block(args):
        q_blk, c_blk, i = args
        s = jnp.einsum('bqhd,bkhd->bhqk', q_blk, k).astype(jnp.float32) * scale
        s = s + c_blk[..., None] - c_k[:, :, None, :]
        q_pos = i * BLOCK + jnp.arange(BLOCK)
        mask = k_pos[None, :] <= q_pos[:, None]
        s = jnp.where(mask[None, None], s, NEG_INF)
        p = jax.nn.softmax(s, axis=-1)
        return jnp.einsum('bhqk,bkhd->bqhd', p.astype(v.dtype), v)

    out = lax.map(one_block, (qb, cq, jnp.arange(nb)))
    return jnp.transpose(out, (1, 0, 2, 3, 4)).reshape(B, S, H * d)


def token_mixer(h, w_in, b_in, sinks, w_proj_a, w_proj_b, w_out):
    B, S, _ = h.shape
    z = h @ w_in + b_in
    idx = [int(i) for i in np.cumsum(SPLIT_SIZES)[:-1]]
    q_a, k_a, v_a, q_b, k_b, v_b, f_logit, g_a, g_b = jnp.split(z, idx, axis=-1)
    y_a = swa_sink_attention(q_a.reshape(B, S, SWA_Q_HEADS, HEAD_DIM),
                             k_a.reshape(B, S, SWA_KV_HEADS, HEAD_DIM),
                             v_a.reshape(B, S, SWA_KV_HEADS, HEAD_DIM), sinks) @ w_proj_a
    log_f = jax.nn.log_sigmoid(f_logit.astype(jnp.float32))
    y_b = forgetting_attention(q_b.reshape(B, S, FOX_HEADS, HEAD_DIM),
                               k_b.reshape(B, S, FOX_HEADS, HEAD_DIM),
                               v_b.reshape(B, S, FOX_HEADS, HEAD_DIM), log_f) @ w_proj_b
    merged = jax.nn.sigmoid(g_a) * y_a + jax.nn.sigmoid(g_b) * y_b
    return merged @ w_out


def conv_gated_ffn(h, w_ffn_in, conv_w, conv_b, w_ffn_out):
    S = h.shape[1]
    gate, up = jnp.split(h @ w_ffn_in, 2, axis=-1)
    gp = jnp.pad(gate, ((0, 0), (CONV_WIDTH - 1, 0), (0, 0)))
    conv = conv_b
    for j in range(CONV_WIDTH):
        conv = conv + conv_w[j] * gp[:, j:j + S]
    return (jax.nn.silu(conv) * up) @ w_ffn_out


def setup_inputs(seed: int = 0) -> dict:
    key = jax.random.key(seed)
    ks = jax.random.split(key, 16)
    f32 = jnp.float32
    L = DEPTH
    beta = DEEPNORM_BETA
    x = jax.random.normal(ks[0], (BATCH, SEQ, D_MODEL), f32)
    ln_mix_g = 1.0 + 0.02 * jax.random.normal(ks[1], (L, D_MODEL), f32)
    ln_mix_b = 0.02 * jax.random.normal(ks[2], (L, D_MODEL), f32)
    col_scale = np.ones((N_IN,), np.float32)
    va0 = SWA_Q + SWA_KV
    col_scale[va0:va0 + SWA_KV] = beta
    vb0 = SWA_Q + 2 * SWA_KV + 2 * FOX_W
    col_scale[vb0:vb0 + FOX_W] = beta
    w_in = jax.random.normal(ks[3], (L, D_MODEL, N_IN), f32) * (D_MODEL ** -0.5) * jnp.asarray(col_scale)
    bias_off = np.zeros((N_IN,), np.float32)
    f0 = SWA_Q + 2 * SWA_KV + 3 * FOX_W
    bias_off[f0:f0 + FOX_HEADS] = FOX_GATE_BIAS_INIT
    b_in = 0.02 * jax.random.normal(ks[4], (L, N_IN), f32) + jnp.asarray(bias_off)
    attn_sinks = 0.5 * jax.random.normal(ks[5], (L, SWA_Q_HEADS), f32)
    w_proj_a = jax.random.normal(ks[6], (L, SWA_Q, D_MODEL), f32) * (SWA_Q ** -0.5) * beta
    w_proj_b = jax.random.normal(ks[7], (L, FOX_W, D_MODEL), f32) * (FOX_W ** -0.5) * beta
    w_out = jax.random.normal(ks[8], (L, D_MODEL, D_MODEL), f32) * (D_MODEL ** -0.5) * beta
    ln_ffn_g = 1.0 + 0.02 * jax.random.normal(ks[9], (L, D_MODEL), f32)
    ln_ffn_b = 0.02 * jax.random.normal(ks[10], (L, D_MODEL), f32)
    w_ffn_in = jax.random.normal(ks[11], (L, D_MODEL, 2 * D_FF), f32) * (D_MODEL ** -0.5) * beta
    conv_w = jax.random.normal(ks[12], (L, CONV_WIDTH, D_FF), f32) * (CONV_WIDTH ** -0.5)
    conv_b = 0.02 * jax.random.normal(ks[13], (L, D_FF), f32)
    w_ffn_out = jax.random.normal(ks[14], (L, D_FF, D_MODEL), f32) * (D_FF ** -0.5) * beta
    return {"x": x, "ln_mix_g": ln_mix_g, "ln_mix_b": ln_mix_b, "w_in": w_in, "b_in": b_in,
            "attn_sinks": attn_sinks, "w_proj_a": w_proj_a, "w_proj_b": w_proj_b, "w_out": w_out,
            "ln_ffn_g": ln_ffn_g, "ln_ffn_b": ln_ffn_b, "w_ffn_in": w_ffn_in, "conv_w": conv_w,
            "conv_b": conv_b, "w_ffn_out": w_ffn_out}


def reference(x, ln_mix_g, ln_mix_b, w_in, b_in, attn_sinks, w_proj_a, w_proj_b, w_out,
              ln_ffn_g, ln_ffn_b, w_ffn_in, conv_w, conv_b, w_ffn_out):
    h = x
    for l in range(DEPTH):
        mix = token_mixer(h, w_in[l], b_in[l], attn_sinks[l], w_proj_a[l], w_proj_b[l], w_out[l])
        h = layer_norm(DEEPNORM_ALPHA * h + mix, ln_mix_g[l], ln_mix_b[l])
        ffn = conv_gated_ffn(h, w_ffn_in[l], conv_w[l], conv_b[l], w_ffn_out[l])
        h = layer_norm(DEEPNORM_ALPHA * h + ffn, ln_ffn_g[l], ln_ffn_b[l])
    return h
```

```python
import functools

import jax
import jax.numpy as jnp
from jax import lax
from jax.experimental import pallas as pl
from jax.experimental.pallas import tpu as pltpu

D_MODEL = 1024
HEAD_DIM = 64
SWA_Q_HEADS = 8
SWA_KV_HEADS = 2
SWA_GROUP = SWA_Q_HEADS // SWA_KV_HEADS
SWA_WINDOW = 128
FOX_HEADS = 8
BLOCK = 128
D_FF = 2816
CONV_WIDTH = 3
LN_EPS = 1e-5
NEG_INF = -1e30
SWA_Q = SWA_Q_HEADS * HEAD_DIM
SWA_KV = SWA_KV_HEADS * HEAD_DIM
FOX_W = FOX_HEADS * HEAD_DIM

LANES = 128
F_ROWS = 16
AUG = 128
V_ROWS = HEAD_DIM + 16
VMEM_LIMIT = 56 * 1024 * 1024

ROW_TILE = 512
SWA_TILE = 512
FOX_Q_TILE = 256
FOX_K_TILE = 1024

F32 = jnp.float32
BF16 = jnp.bfloat16


def _compiler_params(n_axes):
    return pltpu.CompilerParams(dimension_semantics=("arbitrary",) * n_axes,
                                vmem_limit_bytes=VMEM_LIMIT)


def _layer_norm(r, g, b):
    mu = jnp.mean(r, axis=-1, keepdims=True)
    d = r - mu
    var = jnp.mean(d * d, axis=-1, keepdims=True)
    return d * lax.rsqrt(var + LN_EPS) * g + b


def _inproj_kernel(x_ref, wtok_ref, btok_ref, wfm_ref, bfm_ref,
                   ka_ref, ga_ref, gb_ref, qat_ref, vat_ref, qbt_ref, kbt_ref, vbt_ref, ct_ref,
                   carry_ref, *, tm):
    i = pl.program_id(1)
    xb = x_ref[...].astype(BF16)
    zt = jnp.dot(xb, wtok_ref[...], preferred_element_type=F32) + btok_ref[...]
    ka_ref[...] = zt[:, :SWA_KV].astype(BF16)
    ga_ref[...] = zt[:, SWA_KV:SWA_KV + D_MODEL].astype(BF16)
    gb_ref[...] = zt[:, SWA_KV + D_MODEL:].astype(BF16)

    zf = lax.dot_general(wfm_ref[...], xb, (((1,), (1,)), ((), ())), preferred_element_type=F32)
    zf = zf + jnp.concatenate([bfm_ref[...]] * (tm // LANES), axis=1)
    o = 0
    qat_ref[...] = zf[o:o + SWA_Q].astype(BF16); o += SWA_Q
    vat_ref[...] = zf[o:o + SWA_KV].astype(BF16); o += SWA_KV
    qbt_ref[...] = zf[o:o + FOX_W].astype(BF16); o += FOX_W
    kbt_ref[...] = zf[o:o + FOX_W].astype(BF16); o += FOX_W
    vbt_ref[...] = zf[o:o + FOX_W].astype(BF16); o += FOX_W

    f = zf[o:o + FOX_HEADS]
    lf = jnp.minimum(f, 0.0) - jnp.log1p(jnp.exp(-jnp.abs(f)))
    lane = lax.broadcasted_iota(jnp.int32, lf.shape, 1)
    sh = 1
    while sh < tm:
        lf = lf + jnp.where(lane >= sh, pltpu.roll(lf, sh, 1), 0.0)
        sh *= 2

    @pl.when(i == 0)
    def _():
        carry_ref[...] = jnp.zeros_like(carry_ref)

    c = lf + carry_ref[:, 0:1]
    ct_ref[...] = c
    carry_ref[...] = jnp.broadcast_to(c[:, tm - 1:tm], carry_ref.shape)


def _inproj(h, wtok, btok, wfm, bfm, *, B, S, tm):
    T = B * S
    ns = S // tm
    n_tok = wtok.shape[1]
    n_fm = wfm.shape[0]
    row = lambda b, i: (b * ns + i, 0)
    fm = lambda b, i: (b, 0, i)
    const = lambda b, i: (0, 0)
    out_shape = (
        jax.ShapeDtypeStruct((T, SWA_KV), BF16),
        jax.ShapeDtypeStruct((T, D_MODEL), BF16),
        jax.ShapeDtypeStruct((T, D_MODEL), BF16),
        jax.ShapeDtypeStruct((B, SWA_Q, S), BF16),
        jax.ShapeDtypeStruct((B, SWA_KV, S), BF16),
        jax.ShapeDtypeStruct((B, FOX_W, S), BF16),
        jax.ShapeDtypeStruct((B, FOX_W, S), BF16),
        jax.ShapeDtypeStruct((B, FOX_W, S), BF16),
        jax.ShapeDtypeStruct((B, FOX_HEADS, S), F32),
    )
    out_specs = (
        pl.BlockSpec((tm, SWA_KV), row),
        pl.BlockSpec((tm, D_MODEL), row),
        pl.BlockSpec((tm, D_MODEL), row),
        pl.BlockSpec((None, SWA_Q, tm), fm),
        pl.BlockSpec((None, SWA_KV, tm), fm),
        pl.BlockSpec((None, FOX_W, tm), fm),
        pl.BlockSpec((None, FOX_W, tm), fm),
        pl.BlockSpec((None, FOX_W, tm), fm),
        pl.BlockSpec((None, FOX_HEADS, tm), fm),
    )
    in_specs = [
        pl.BlockSpec((tm, D_MODEL), row),
        pl.BlockSpec((D_MODEL, n_tok), const),
        pl.BlockSpec((1, n_tok), const),
        pl.BlockSpec((n_fm, D_MODEL), const),
        pl.BlockSpec((n_fm, LANES), const),
    ]
    return pl.pallas_call(
        functools.partial(_inproj_kernel, tm=tm),
        grid=(B, ns),
        in_specs=in_specs,
        out_specs=out_specs,
        out_shape=out_shape,
        scratch_shapes=[pltpu.VMEM((FOX_HEADS, LANES), F32)],
        compiler_params=_compiler_params(2),
        name="inproj",
    )(h, wtok, btok, wfm, bfm)


def _swa_kernel(qat_ref, ka_ref, kap_ref, vat_ref, vap_ref, sink_ref, o_ref, bias_ref, *, ts):
    b = pl.program_id(0)
    i = pl.program_id(1)
    nblk = ts // BLOCK
    gw = SWA_GROUP * BLOCK

    @pl.when((b == 0) & (i == 0))
    def _():
        r = lax.broadcasted_iota(jnp.int32, (2 * BLOCK, BLOCK), 0)
        c = lax.broadcasted_iota(jnp.int32, (2 * BLOCK, BLOCK), 1)
        dist = BLOCK + c - r
        valid = (dist >= 0) & (dist < SWA_WINDOW)
        distf = dist.astype(F32)
        for g in range(SWA_KV_HEADS):
            for hh in range(SWA_GROUP):
                slope = 2.0 ** (-8.0 * (g * SWA_GROUP + hh + 1) / SWA_Q_HEADS)
                bias_ref[g, :, hh * BLOCK:(hh + 1) * BLOCK] = jnp.where(valid, -slope * distf, NEG_INF)

    kext = jnp.concatenate([kap_ref[...], ka_ref[...]], axis=0)
    vext = jnp.concatenate([vap_ref[...], vat_ref[...]], axis=1)
    zeros_q = jnp.zeros((HEAD_DIM, gw), BF16)
    for n in range(nblk):
        k_win = kext[n * BLOCK:(n + 2) * BLOCK]
        for g in range(SWA_KV_HEADS):
            qrow = jnp.concatenate(
                [qat_ref[(g * SWA_GROUP + hh) * HEAD_DIM:(g * SWA_GROUP + hh + 1) * HEAD_DIM,
                         n * BLOCK:(n + 1) * BLOCK] for hh in range(SWA_GROUP)], axis=1)
            q_pad = jnp.concatenate([qrow, zeros_q] if g == 0 else [zeros_q, qrow], axis=0)
            s = jnp.dot(k_win, q_pad, preferred_element_type=F32) + bias_ref[g]
            if n == 0:
                top = jnp.where(i == 0, NEG_INF, s[:BLOCK])
                s = jnp.concatenate([top, s[BLOCK:]], axis=0)
            sink = sink_ref[g]
            m = jnp.maximum(jnp.max(s, axis=0, keepdims=True), sink)
            p = jnp.exp(s - m)
            denom = jnp.sum(p, axis=0, keepdims=True) + jnp.exp(sink - m)
            v_win = vext[g * HEAD_DIM:(g + 1) * HEAD_DIM, n * BLOCK:(n + 2) * BLOCK]
            out = jnp.dot(v_win, p.astype(BF16), preferred_element_type=F32) / denom
            for hh in range(SWA_GROUP):
                h0 = (g * SWA_GROUP + hh) * HEAD_DIM
                o_ref[h0:h0 + HEAD_DIM, n * BLOCK:(n + 1) * BLOCK] = (
                    out[:, hh * BLOCK:(hh + 1) * BLOCK].astype(BF16))


def _swa(qat, ka, vat, sink_rows, *, B, S, ts):
    ns = S // ts
    nblk = ts // BLOCK
    ka3 = ka.reshape(B, S, SWA_KV)
    prev = lambda b, i: jnp.maximum(i * nblk - 1, 0)
    in_specs = [
        pl.BlockSpec((None, SWA_Q, ts), lambda b, i: (b, 0, i)),
        pl.BlockSpec((None, ts, SWA_KV), lambda b, i: (b, i, 0)),
        pl.BlockSpec((None, BLOCK, SWA_KV), lambda b, i: (b, prev(b, i), 0)),
        pl.BlockSpec((None, SWA_KV, ts), lambda b, i: (b, 0, i)),
        pl.BlockSpec((None, SWA_KV, BLOCK), lambda b, i: (b, 0, prev(b, i))),
        pl.BlockSpec((SWA_KV_HEADS, 1, SWA_GROUP * BLOCK), lambda b, i: (0, 0, 0)),
    ]
    return pl.pallas_call(
        functools.partial(_swa_kernel, ts=ts),
        grid=(B, ns),
        in_specs=in_specs,
        out_specs=pl.BlockSpec((None, SWA_Q, ts), lambda b, i: (b, 0, i)),
        out_shape=jax.ShapeDtypeStruct((B, SWA_Q, S), BF16),
        scratch_shapes=[pltpu.VMEM((SWA_KV_HEADS, 2 * BLOCK, SWA_GROUP * BLOCK), F32)],
        compiler_params=_compiler_params(2),
        name="swa",
    )(qat, ka3, ka3, vat, vat, sink_rows)


def _split3(c):
    hi = c.astype(BF16).astype(F32)
    r = c - hi
    mid = r.astype(BF16).astype(F32)
    lo = (r - mid).astype(BF16).astype(F32)
    return hi, mid, lo


def _rows8(a, b, c, n):
    rid = lax.broadcasted_iota(jnp.int32, (8, n), 0)
    return jnp.where(rid == 0, a, jnp.where(rid == 1, b, jnp.where(rid == 2, c, 0.0)))


def _fox_kernel(qt_ref, kt_ref, vt_ref, c_ref, cq_ref, o_ref, kaug_ref, vaug_ref, *, S, tq, tk):
    i = pl.program_id(2)
    nkt = S // tk
    pad_rows = AUG - HEAD_DIM - 16

    @pl.when(i == 0)
    def _():
        for t in range(nkt):
            sl = slice(t * tk, (t + 1) * tk)
            hi, mid, lo = _split3(c_ref[:, sl])
            x = jnp.concatenate([kt_ref[:, sl].astype(F32), _rows8(1.0, 1.0, 1.0, tk),
                                 _rows8(-hi, -mid, -lo, tk), jnp.zeros((pad_rows, tk), F32)], axis=0)
            kaug_ref[t] = x.T.astype(BF16)
            rid = lax.broadcasted_iota(jnp.int32, (V_ROWS - HEAD_DIM, tk), 0)
            vaug_ref[t] = jnp.concatenate(
                [vt_ref[:, sl], jnp.where(rid == 0, 1.0, 0.0).astype(BF16)], axis=0)

    hi, mid, lo = _split3(cq_ref[...])
    q_aug = jnp.concatenate([qt_ref[...].astype(F32), _rows8(hi, mid, lo, tq),
                             _rows8(1.0, 1.0, 1.0, tq), jnp.zeros((pad_rows, tq), F32)],
                            axis=0).astype(BF16)

    def tile(j, carry, masked):
        m, acc = carry
        s = jnp.dot(kaug_ref[j], q_aug, preferred_element_type=F32)
        if masked:
            kidx = j * tk + lax.broadcasted_iota(jnp.int32, (tk, tq), 0)
            qidx = i * tq + lax.broadcasted_iota(jnp.int32, (tk, tq), 1)
            s = jnp.where(kidx <= qidx, s, NEG_INF)
        m_new = jnp.maximum(m, jnp.max(s, axis=0, keepdims=True))
        alpha = jnp.exp(m - m_new)
        p = jnp.exp(s - m_new).astype(BF16)
        acc = acc * alpha + jnp.dot(vaug_ref[j], p, preferred_element_type=F32)
        return m_new, acc

    nfull = (i * tq) // tk
    carry = (jnp.full((1, tq), NEG_INF, F32), jnp.zeros((V_ROWS, tq), F32))
    carry = lax.fori_loop(0, nfull, lambda j, cr: tile(j, cr, False), carry)
    _, acc = tile(nfull, carry, True)
    o_ref[...] = (acc[:HEAD_DIM] / acc[HEAD_DIM:HEAD_DIM + 1]).astype(BF16)


def _fox(qbt, kbt, vbt, ct, *, B, S, tq, tk):
    nq = S // tq
    c3 = ct.reshape(B * FOX_HEADS, 1, S)
    in_specs = [
        pl.BlockSpec((None, HEAD_DIM, tq), lambda b, h, i: (b, h, i)),
        pl.BlockSpec((None, HEAD_DIM, S), lambda b, h, i: (b, h, 0)),
        pl.BlockSpec((None, HEAD_DIM, S), lambda b, h, i: (b, h, 0)),
        pl.BlockSpec((None, 1, S), lambda b, h, i: (b * FOX_HEADS + h, 0, 0)),
        pl.BlockSpec((None, 1, tq), lambda b, h, i: (b * FOX_HEADS + h, 0, i)),
    ]
    return pl.pallas_call(
        functools.partial(_fox_kernel, S=S, tq=tq, tk=tk),
        grid=(B, FOX_HEADS, nq),
        in_specs=in_specs,
        out_specs=pl.BlockSpec((None, HEAD_DIM, tq), lambda b, h, i: (b, h, i)),
        out_shape=jax.ShapeDtypeStruct((B, FOX_W, S), BF16),
        scratch_shapes=[pltpu.VMEM((S // tk, tk, AUG), BF16),
                        pltpu.VMEM((S // tk, V_ROWS, tk), BF16)],
        compiler_params=_compiler_params(3),
        name="fox",
    )(qbt, kbt, vbt, c3, c3)


def _mix_kernel(h_ref, yat_ref, ybt_ref, ga_ref, gb_ref, wpa_ref, wpb_ref, wout_ref, lng_ref, lnb_ref,
                o_ref, *, alpha):
    tn = (((0,), (0,)), ((), ()))
    ya = lax.dot_general(yat_ref[...], wpa_ref[...], tn, preferred_element_type=F32)
    yb = lax.dot_general(ybt_ref[...], wpb_ref[...], tn, preferred_element_type=F32)
    merged = (jax.nn.sigmoid(ga_ref[...].astype(F32)) * ya
              + jax.nn.sigmoid(gb_ref[...].astype(F32)) * yb)
    mix = jnp.dot(merged.astype(BF16), wout_ref[...], preferred_element_type=F32)
    o_ref[...] = _layer_norm(alpha * h_ref[...] + mix, lng_ref[...], lnb_ref[...])


def _mix(h, yat, ybt, ga, gb, wpa, wpb, wout, lng, lnb, *, B, S, tm, alpha):
    T = B * S
    ns = S // tm
    row = lambda b, i: (b * ns + i, 0)
    fm = lambda b, i: (b, 0, i)
    const = lambda b, i: (0, 0)
    in_specs = [
        pl.BlockSpec((tm, D_MODEL), row),
        pl.BlockSpec((None, SWA_Q, tm), fm),
        pl.BlockSpec((None, FOX_W, tm), fm),
        pl.BlockSpec((tm, D_MODEL), row),
        pl.BlockSpec((tm, D_MODEL), row),
        pl.BlockSpec((SWA_Q, D_MODEL), const),
        pl.BlockSpec((FOX_W, D_MODEL), const),
        pl.BlockSpec((D_MODEL, D_MODEL), const),
        pl.BlockSpec((1, D_MODEL), const),
        pl.BlockSpec((1, D_MODEL), const),
    ]
    return pl.pallas_call(
        functools.partial(_mix_kernel, alpha=alpha),
        grid=(B, ns),
        in_specs=in_specs,
        out_specs=pl.BlockSpec((tm, D_MODEL), row),
        out_shape=jax.ShapeDtypeStruct((T, D_MODEL), F32),
        compiler_params=_compiler_params(2),
        name="mix",
    )(h, yat, ybt, ga, gb, wpa, wpb, wout, lng, lnb)


FFN_HALO = 16
FFN_CHUNK = 256


def _ffn_kernel(h_ref, hp_ref, wg_ref, wu_ref, cw_ref, cb_ref, wo_ref, lng_ref, lnb_ref, o_ref, a_ref,
                *, tm, alpha):
    i = pl.program_id(1)
    h = h_ref[...]
    hb = h.astype(BF16)
    hp = jnp.where(i == 0, 0.0, hp_ref[...]).astype(BF16)
    hx = jnp.concatenate([hp, hb], axis=0)
    for c in range(D_FF // FFN_CHUNK):
        sl = slice(c * FFN_CHUNK, (c + 1) * FFN_CHUNK)
        g = jnp.dot(hx, wg_ref[:, sl], preferred_element_type=F32)
        u = jnp.dot(hb, wu_ref[:, sl], preferred_element_type=F32)
        conv = cb_ref[:, sl] + cw_ref[2:3, sl] * g[FFN_HALO:]
        conv = conv + cw_ref[1:2, sl] * pltpu.roll(g, 1, 0)[FFN_HALO:]
        conv = conv + cw_ref[0:1, sl] * pltpu.roll(g, 2, 0)[FFN_HALO:]
        a_ref[:, sl] = (conv * jax.nn.sigmoid(conv) * u).astype(BF16)
    ffn = jnp.dot(a_ref[...], wo_ref[...], preferred_element_type=F32)
    o_ref[...] = _layer_norm(alpha * h + ffn, lng_ref[...], lnb_ref[...])


def _ffn(h, wg, wu, cw, cb, wo, lng, lnb, *, B, S, tm, alpha):
    T = B * S
    ns = S // tm
    row = lambda b, i: (b * ns + i, 0)
    halo = lambda b, i: (jnp.maximum((b * ns + i) * (tm // FFN_HALO) - 1, 0), 0)
    const = lambda b, i: (0, 0)
    in_specs = [
        pl.BlockSpec((tm, D_MODEL), row),
        pl.BlockSpec((FFN_HALO, D_MODEL), halo),
        pl.BlockSpec((D_MODEL, D_FF), const),
        pl.BlockSpec((D_MODEL, D_FF), const),
        pl.BlockSpec((CONV_WIDTH, D_FF), const),
        pl.BlockSpec((1, D_FF), const),
        pl.BlockSpec((D_FF, D_MODEL), const),
        pl.BlockSpec((1, D_MODEL), const),
        pl.BlockSpec((1, D_MODEL), const),
    ]
    return pl.pallas_call(
        functools.partial(_ffn_kernel, tm=tm, alpha=alpha),
        grid=(B, ns),
        in_specs=in_specs,
        out_specs=pl.BlockSpec((tm, D_MODEL), row),
        out_shape=jax.ShapeDtypeStruct((T, D_MODEL), F32),
        scratch_shapes=[pltpu.VMEM((tm, D_FF), BF16)],
        compiler_params=_compiler_params(2),
        name="ffn",
    )(h, h, wg, wu, cw, cb, wo, lng, lnb)


def _prep_inproj_weights(w_in, b_in):
    scale = HEAD_DIM ** -0.5
    o = 0
    parts = {}
    for name, n in (("qa", SWA_Q), ("ka", SWA_KV), ("va", SWA_KV), ("qb", FOX_W), ("kb", FOX_W),
                    ("vb", FOX_W), ("f", FOX_HEADS), ("ga", D_MODEL), ("gb", D_MODEL)):
        parts[name] = (w_in[:, o:o + n], b_in[o:o + n])
        o += n
    tok = [parts["ka"], parts["ga"], parts["gb"]]
    wtok = jnp.concatenate([w for w, _ in tok], axis=1).astype(BF16)
    btok = jnp.concatenate([b for _, b in tok])[None, :]
    pad_w = jnp.zeros((D_MODEL, F_ROWS - FOX_HEADS), F32)
    pad_b = jnp.zeros((F_ROWS - FOX_HEADS,), F32)
    fm_w = [parts["qa"][0] * scale, parts["va"][0], parts["qb"][0] * scale, parts["kb"][0],
            parts["vb"][0], parts["f"][0], pad_w]
    fm_b = [parts["qa"][1] * scale, parts["va"][1], parts["qb"][1] * scale, parts["kb"][1],
            parts["vb"][1], parts["f"][1], pad_b]
    wfm = jnp.concatenate(fm_w, axis=1).T.astype(BF16)
    bfm = jnp.broadcast_to(jnp.concatenate(fm_b)[:, None], (wfm.shape[0], LANES))
    return wtok, btok, wfm, bfm


def kernel(x, ln_mix_g, ln_mix_b, w_in, b_in, attn_sinks, w_proj_a, w_proj_b, w_out, ln_ffn_g, ln_ffn_b,
           w_ffn_in, conv_w, conv_b, w_ffn_out):
    B, S, D = x.shape
    assert D == D_MODEL and S % BLOCK == 0
    depth = w_in.shape[0]
    alpha = (2 * depth) ** 0.25
    tm = min(ROW_TILE, S)
    ts = min(SWA_TILE, S)
    tq = min(FOX_Q_TILE, S)
    tk = min(FOX_K_TILE, S)
    assert S % tm == 0 and S % ts == 0 and S % tq == 0 and S % tk == 0 and tk % tq == 0

    h = x.reshape(B * S, D)
    for l in range(depth):
        wtok, btok, wfm, bfm = _prep_inproj_weights(w_in[l], b_in[l])
        ka, ga, gb, qat, vat, qbt, kbt, vbt, ct = _inproj(h, wtok, btok, wfm, bfm, B=B, S=S, tm=tm)
        sink_rows = jnp.broadcast_to(
            attn_sinks[l].reshape(SWA_KV_HEADS, SWA_GROUP, 1), (SWA_KV_HEADS, SWA_GROUP, BLOCK)
        ).reshape(SWA_KV_HEADS, 1, SWA_GROUP * BLOCK)
        yat = _swa(qat, ka, vat, sink_rows, B=B, S=S, ts=ts)
        ybt = _fox(qbt, kbt, vbt, ct, B=B, S=S, tq=tq, tk=tk)
        h = _mix(h, yat, ybt, ga, gb, w_proj_a[l].astype(BF16), w_proj_b[l].astype(BF16),
                 w_out[l].astype(BF16), ln_mix_g[l][None, :], ln_mix_b[l][None, :],
                 B=B, S=S, tm=tm, alpha=alpha)
        h = _ffn(h, w_ffn_in[l][:, :D_FF].astype(BF16), w_ffn_in[l][:, D_FF:].astype(BF16),
                 conv_w[l], conv_b[l][None, :], w_ffn_out[l].astype(BF16),
                 ln_ffn_g[l][None, :], ln_ffn_b[l][None, :], B=B, S=S, tm=tm, alpha=alpha)
    return h.reshape(B, S, D)
```

```python
import functools

import jax
import jax.numpy as jnp
from jax import lax
from jax.experimental import pallas as pl
from jax.experimental.pallas import tpu as pltpu

D_MODEL = 1024
HEAD_DIM = 64
SWA_Q_HEADS = 8
SWA_KV_HEADS = 2
SWA_GROUP = SWA_Q_HEADS // SWA_KV_HEADS
SWA_WINDOW = 128
FOX_HEADS = 8
BLOCK = 128
D_FF = 2816
CONV_WIDTH = 3
LN_EPS = 1e-5
NEG_INF = -1e30
SWA_Q = SWA_Q_HEADS * HEAD_DIM
SWA_KV = SWA_KV_HEADS * HEAD_DIM
FOX_W = FOX_HEADS * HEAD_DIM

LANES = 128
F_ROWS = 16
AUG = 128
V_ROWS = HEAD_DIM + 16
VMEM_LIMIT = 56 * 1024 * 1024

ROW_TILE = 512
SWA_TILE = 512
FOX_Q_TILE = 1024
FOX_K_TILE = 1024

F32 = jnp.float32
BF16 = jnp.bfloat16


def _compiler_params(n_axes):
    return pltpu.CompilerParams(dimension_semantics=("arbitrary",) * n_axes,
                                vmem_limit_bytes=VMEM_LIMIT)


def _layer_norm(r, g, b):
    mu = jnp.mean(r, axis=-1, keepdims=True)
    d = r - mu
    var = jnp.mean(d * d, axis=-1, keepdims=True)
    return d * lax.rsqrt(var + LN_EPS) * g + b


def _inproj_kernel(x_ref, wtok_ref, btok_ref, wfm_ref, bfm_ref,
                   ka_ref, ga_ref, gb_ref, qat_ref, vat_ref, qbt_ref, kbt_ref, vbt_ref, ct_ref,
                   carry_ref, *, tm):
    i = pl.program_id(1)
    xb = x_ref[...].astype(BF16)
    zt = jnp.dot(xb, wtok_ref[...], preferred_element_type=F32) + btok_ref[...]
    ka_ref[...] = zt[:, :SWA_KV].astype(BF16)
    ga_ref[...] = zt[:, SWA_KV:SWA_KV + D_MODEL].astype(BF16)
    gb_ref[...] = zt[:, SWA_KV + D_MODEL:].astype(BF16)

    zf = lax.dot_general(wfm_ref[...], xb, (((1,), (1,)), ((), ())), preferred_element_type=F32)
    zf = zf + jnp.concatenate([bfm_ref[...]] * (tm // LANES), axis=1)
    o = 0
    qat_ref[...] = zf[o:o + SWA_Q].astype(BF16); o += SWA_Q
    vat_ref[...] = zf[o:o + SWA_KV].astype(BF16); o += SWA_KV
    qbt_ref[...] = zf[o:o + FOX_W].astype(BF16); o += FOX_W
    kbt_ref[...] = zf[o:o + FOX_W].astype(BF16); o += FOX_W
    vbt_ref[...] = zf[o:o + FOX_W].astype(BF16); o += FOX_W

    f = zf[o:o + FOX_HEADS]
    lf = jnp.minimum(f, 0.0) - jnp.log1p(jnp.exp(-jnp.abs(f)))
    lane = lax.broadcasted_iota(jnp.int32, lf.shape, 1)
    sh = 1
    while sh < tm:
        lf = lf + jnp.where(lane >= sh, pltpu.roll(lf, sh, 1), 0.0)
        sh *= 2

    @pl.when(i == 0)
    def _():
        carry_ref[...] = jnp.zeros_like(carry_ref)

    c = lf + carry_ref[:, 0:1]
    ct_ref[...] = c
    carry_ref[...] = jnp.broadcast_to(c[:, tm - 1:tm], carry_ref.shape)


def _inproj(h, wtok, btok, wfm, bfm, *, B, S, tm):
    T = B * S
    ns = S // tm
    n_tok = wtok.shape[1]
    n_fm = wfm.shape[0]
    row = lambda b, i: (b * ns + i, 0)
    fm = lambda b, i: (b, 0, i)
    const = lambda b, i: (0, 0)
    out_shape = (
        jax.ShapeDtypeStruct((T, SWA_KV), BF16),
        jax.ShapeDtypeStruct((T, D_MODEL), BF16),
        jax.ShapeDtypeStruct((T, D_MODEL), BF16),
        jax.ShapeDtypeStruct((B, SWA_Q, S), BF16),
        jax.ShapeDtypeStruct((B, SWA_KV, S), BF16),
        jax.ShapeDtypeStruct((B, FOX_W, S), BF16),
        jax.ShapeDtypeStruct((B, FOX_W, S), BF16),
        jax.ShapeDtypeStruct((B, FOX_W, S), BF16),
        jax.ShapeDtypeStruct((B, FOX_HEADS, S), F32),
    )
    out_specs = (
        pl.BlockSpec((tm, SWA_KV), row),
        pl.BlockSpec((tm, D_MODEL), row),
        pl.BlockSpec((tm, D_MODEL), row),
        pl.BlockSpec((None, SWA_Q, tm), fm),
        pl.BlockSpec((None, SWA_KV, tm), fm),
        pl.BlockSpec((None, FOX_W, tm), fm),
        pl.BlockSpec((None, FOX_W, tm), fm),
        pl.BlockSpec((None, FOX_W, tm), fm),
        pl.BlockSpec((None, FOX_HEADS, tm), fm),
    )
    in_specs = [
        pl.BlockSpec((tm, D_MODEL), row),
        pl.BlockSpec((D_MODEL, n_tok), const),
        pl.BlockSpec((1, n_tok), const),
        pl.BlockSpec((n_fm, D_MODEL), const),
        pl.BlockSpec((n_fm, LANES), const),
    ]
    return pl.pallas_call(
        functools.partial(_inproj_kernel, tm=tm),
        grid=(B, ns),
        in_specs=in_specs,
        out_specs=out_specs,
        out_shape=out_shape,
        scratch_shapes=[pltpu.VMEM((FOX_HEADS, LANES), F32)],
        compiler_params=_compiler_params(2),
        name="inproj",
    )(h, wtok, btok, wfm, bfm)


def _swa_kernel(qat_ref, ka_ref, kap_ref, vat_ref, vap_ref, sink_ref, o_ref, bias_ref, *, ts):
    b = pl.program_id(0)
    i = pl.program_id(1)
    nblk = ts // BLOCK
    gw = SWA_GROUP * BLOCK

    @pl.when((b == 0) & (i == 0))
    def _():
        r = lax.broadcasted_iota(jnp.int32, (2 * BLOCK, BLOCK), 0)
        c = lax.broadcasted_iota(jnp.int32, (2 * BLOCK, BLOCK), 1)
        dist = BLOCK + c - r
        valid = (dist >= 0) & (dist < SWA_WINDOW)
        distf = dist.astype(F32)
        for g in range(SWA_KV_HEADS):
            for hh in range(SWA_GROUP):
                slope = 2.0 ** (-8.0 * (g * SWA_GROUP + hh + 1) / SWA_Q_HEADS)
                bias_ref[g, :, hh * BLOCK:(hh + 1) * BLOCK] = jnp.where(valid, -slope * distf, NEG_INF)

    kext = jnp.concatenate([kap_ref[...], ka_ref[...]], axis=0)
    vext = jnp.concatenate([vap_ref[...], vat_ref[...]], axis=1)
    zeros_q = jnp.zeros((HEAD_DIM, gw), BF16)
    for n in range(nblk):
        k_win = kext[n * BLOCK:(n + 2) * BLOCK]
        for g in range(SWA_KV_HEADS):
            qrow = jnp.concatenate(
                [qat_ref[(g * SWA_GROUP + hh) * HEAD_DIM:(g * SWA_GROUP + hh + 1) * HEAD_DIM,
                         n * BLOCK:(n + 1) * BLOCK] for hh in range(SWA_GROUP)], axis=1)
            q_pad = jnp.concatenate([qrow, zeros_q] if g == 0 else [zeros_q, qrow], axis=0)
            s = jnp.dot(k_win, q_pad, preferred_element_type=F32) + bias_ref[g]
            if n == 0:
                top = jnp.where(i == 0, NEG_INF, s[:BLOCK])
                s = jnp.concatenate([top, s[BLOCK:]], axis=0)
            sink = sink_ref[g]
            m = jnp.maximum(jnp.max(s, axis=0, keepdims=True), sink)
            p = jnp.exp(s - m)
            denom = jnp.sum(p, axis=0, keepdims=True) + jnp.exp(sink - m)
            v_win = vext[g * HEAD_DIM:(g + 1) * HEAD_DIM, n * BLOCK:(n + 2) * BLOCK]
            out = jnp.dot(v_win, p.astype(BF16), preferred_element_type=F32) / denom
            for hh in range(SWA_GROUP):
                h0 = (g * SWA_GROUP + hh) * HEAD_DIM
                o_ref[h0:h0 + HEAD_DIM, n * BLOCK:(n + 1) * BLOCK] = (
                    out[:, hh * BLOCK:(hh + 1) * BLOCK].astype(BF16))


def _swa(qat, ka, vat, sink_rows, *, B, S, ts):
    ns = S // ts
    nblk = ts // BLOCK
    ka3 = ka.reshape(B, S, SWA_KV)
    prev = lambda b, i: jnp.maximum(i * nblk - 1, 0)
    in_specs = [
        pl.BlockSpec((None, SWA_Q, ts), lambda b, i: (b, 0, i)),
        pl.BlockSpec((None, ts, SWA_KV), lambda b, i: (b, i, 0)),
        pl.BlockSpec((None, BLOCK, SWA_KV), lambda b, i: (b, prev(b, i), 0)),
        pl.BlockSpec((None, SWA_KV, ts), lambda b, i: (b, 0, i)),
        pl.BlockSpec((None, SWA_KV, BLOCK), lambda b, i: (b, 0, prev(b, i))),
        pl.BlockSpec((SWA_KV_HEADS, 1, SWA_GROUP * BLOCK), lambda b, i: (0, 0, 0)),
    ]
    return pl.pallas_call(
        functools.partial(_swa_kernel, ts=ts),
        grid=(B, ns),
        in_specs=in_specs,
        out_specs=pl.BlockSpec((None, SWA_Q, ts), lambda b, i: (b, 0, i)),
        out_shape=jax.ShapeDtypeStruct((B, SWA_Q, S), BF16),
        scratch_shapes=[pltpu.VMEM((SWA_KV_HEADS, 2 * BLOCK, SWA_GROUP * BLOCK), F32)],
        compiler_params=_compiler_params(2),
        name="swa",
    )(qat, ka3, ka3, vat, vat, sink_rows)


def _split3(c):
    hi = c.astype(BF16).astype(F32)
    r = c - hi
    mid = r.astype(BF16).astype(F32)
    lo = (r - mid).astype(BF16).astype(F32)
    return hi, mid, lo


def _rows8(a, b, c, n):
    rid = lax.broadcasted_iota(jnp.int32, (8, n), 0)
    return jnp.where(rid == 0, a, jnp.where(rid == 1, b, jnp.where(rid == 2, c, 0.0)))


def _fox_kernel(qt_ref, kt_ref, vt_ref, c_ref, cq_ref, o_ref, kaug_ref, vaug_ref, *, S, tq, tk):
    i = pl.program_id(2)
    nkt = S // tk
    pad_rows = AUG - HEAD_DIM - 16

    @pl.when(i == 0)
    def _():
        for t in range(nkt):
            sl = slice(t * tk, (t + 1) * tk)
            hi, mid, lo = _split3(c_ref[:, sl])
            x = jnp.concatenate([kt_ref[:, sl].astype(F32), _rows8(1.0, 1.0, 1.0, tk),
                                 _rows8(-hi, -mid, -lo, tk), jnp.zeros((pad_rows, tk), F32)], axis=0)
            kaug_ref[t] = x.T.astype(BF16)
            rid = lax.broadcasted_iota(jnp.int32, (V_ROWS - HEAD_DIM, tk), 0)
            vaug_ref[t] = jnp.concatenate(
                [vt_ref[:, sl], jnp.where(rid == 0, 1.0, 0.0).astype(BF16)], axis=0)

    hi, mid, lo = _split3(cq_ref[...])
    q_aug = jnp.concatenate([qt_ref[...].astype(F32), _rows8(hi, mid, lo, tq),
                             _rows8(1.0, 1.0, 1.0, tq), jnp.zeros((pad_rows, tq), F32)],
                            axis=0).astype(BF16)

    def tile(j, carry, masked):
        m, acc = carry
        s = jnp.dot(kaug_ref[j], q_aug, preferred_element_type=F32)
        if masked:
            kidx = j * tk + lax.broadcasted_iota(jnp.int32, (tk, tq), 0)
            qidx = i * tq + lax.broadcasted_iota(jnp.int32, (tk, tq), 1)
            s = jnp.where(kidx <= qidx, s, NEG_INF)
        m_new = jnp.maximum(m, jnp.max(s, axis=0, keepdims=True))
        alpha = jnp.exp(m - m_new)
        p = jnp.exp(s - m_new).astype(BF16)
        acc = acc * alpha + jnp.dot(vaug_ref[j], p, preferred_element_type=F32)
        return m_new, acc

    nfull = (i * tq) // tk
    carry = (jnp.full((1, tq), NEG_INF, F32), jnp.zeros((V_ROWS, tq), F32))
    carry = lax.fori_loop(0, nfull, lambda j, cr: tile(j, cr, False), carry)
    _, acc = tile(nfull, carry, True)
    o_ref[...] = (acc[:HEAD_DIM] / acc[HEAD_DIM:HEAD_DIM + 1]).astype(BF16)


def _fox(qbt, kbt, vbt, ct, *, B, S, tq, tk):
    nq = S // tq
    c3 = ct.reshape(B * FOX_HEADS, 1, S)
    in_specs = [
        pl.BlockSpec((None, HEAD_DIM, tq), lambda b, h, i: (b, h, i)),
        pl.BlockSpec((None, HEAD_DIM, S), lambda b, h, i: (b, h, 0)),
        pl.BlockSpec((None, HEAD_DIM, S), lambda b, h, i: (b, h, 0)),
        pl.BlockSpec((None, 1, S), lambda b, h, i: (b * FOX_HEADS + h, 0, 0)),
        pl.BlockSpec((None, 1, tq), lambda b, h, i: (b * FOX_HEADS + h, 0, i)),
    ]
    return pl.pallas_call(
        functools.partial(_fox_kernel, S=S, tq=tq, tk=tk),
        grid=(B, FOX_HEADS, nq),
        in_specs=in_specs,
        out_specs=pl.BlockSpec((None, HEAD_DIM, tq), lambda b, h, i: (b, h, i)),
        out_shape=jax.ShapeDtypeStruct((B, FOX_W, S), BF16),
        scratch_shapes=[pltpu.VMEM((S // tk, tk, AUG), BF16),
                        pltpu.VMEM((S // tk, V_ROWS, tk), BF16)],
        compiler_params=_compiler_params(3),
        name="fox",
    )(qbt, kbt, vbt, c3, c3)


def _mix_kernel(h_ref, yat_ref, ybt_ref, ga_ref, gb_ref, wpa_ref, wpb_ref, wout_ref, lng_ref, lnb_ref,
                o_ref, *, alpha):
    tn = (((0,), (0,)), ((), ()))
    ya = lax.dot_general(yat_ref[...], wpa_ref[...], tn, preferred_element_type=F32)
    yb = lax.dot_general(ybt_ref[...], wpb_ref[...], tn, preferred_element_type=F32)
    merged = (jax.nn.sigmoid(ga_ref[...].astype(F32)) * ya
              + jax.nn.sigmoid(gb_ref[...].astype(F32)) * yb)
    mix = jnp.dot(merged.astype(BF16), wout_ref[...], preferred_element_type=F32)
    o_ref[...] = _layer_norm(alpha * h_ref[...] + mix, lng_ref[...], lnb_ref[...])


def _mix(h, yat, ybt, ga, gb, wpa, wpb, wout, lng, lnb, *, B, S, tm, alpha):
    T = B * S
    ns = S // tm
    row = lambda b, i: (b * ns + i, 0)
    fm = lambda b, i: (b, 0, i)
    const = lambda b, i: (0, 0)
    in_specs = [
        pl.BlockSpec((tm, D_MODEL), row),
        pl.BlockSpec((None, SWA_Q, tm), fm),
        pl.BlockSpec((None, FOX_W, tm), fm),
        pl.BlockSpec((tm, D_MODEL), row),
        pl.BlockSpec((tm, D_MODEL), row),
        pl.BlockSpec((SWA_Q, D_MODEL), const),
        pl.BlockSpec((FOX_W, D_MODEL), const),
        pl.BlockSpec((D_MODEL, D_MODEL), const),
        pl.BlockSpec((1, D_MODEL), const),
        pl.BlockSpec((1, D_MODEL), const),
    ]
    return pl.pallas_call(
        functools.partial(_mix_kernel, alpha=alpha),
        grid=(B, ns),
        in_specs=in_specs,
        out_specs=pl.BlockSpec((tm, D_MODEL), row),
        out_shape=jax.ShapeDtypeStruct((T, D_MODEL), F32),
        compiler_params=_compiler_params(2),
        name="mix",
    )(h, yat, ybt, ga, gb, wpa, wpb, wout, lng, lnb)


FFN_HALO = 16
FFN_CHUNK = 256


def _ffn_kernel(h_ref, hp_ref, wg_ref, wu_ref, cw_ref, cb_ref, wo_ref, lng_ref, lnb_ref, o_ref, a_ref,
                *, tm, alpha):
    i = pl.program_id(1)
    h = h_ref[...]
    hb = h.astype(BF16)
    hp = jnp.where(i == 0, 0.0, hp_ref[...]).astype(BF16)
    hx = jnp.concatenate([hp, hb], axis=0)
    for c in range(D_FF // FFN_CHUNK):
        sl = slice(c * FFN_CHUNK, (c + 1) * FFN_CHUNK)
        g = jnp.dot(hx, wg_ref[:, sl], preferred_element_type=F32)
        u = jnp.dot(hb, wu_ref[:, sl], preferred_element_type=F32)
        conv = cb_ref[:, sl] + cw_ref[2:3, sl] * g[FFN_HALO:]
        conv = conv + cw_ref[1:2, sl] * pltpu.roll(g, 1, 0)[FFN_HALO:]
        conv = conv + cw_ref[0:1, sl] * pltpu.roll(g, 2, 0)[FFN_HALO:]
        a_ref[:, sl] = (conv * jax.nn.sigmoid(conv) * u).astype(BF16)
    ffn = jnp.dot(a_ref[...], wo_ref[...], preferred_element_type=F32)
    o_ref[...] = _layer_norm(alpha * h + ffn, lng_ref[...], lnb_ref[...])


def _ffn(h, wg, wu, cw, cb, wo, lng, lnb, *, B, S, tm, alpha):
    T = B * S
    ns = S // tm
    row = lambda b, i: (b * ns + i, 0)
    halo = lambda b, i: (jnp.maximum((b * ns + i) * (tm // FFN_HALO) - 1, 0), 0)
    const = lambda b, i: (0, 0)
    in_specs = [
        pl.BlockSpec((tm, D_MODEL), row),
        pl.BlockSpec((FFN_HALO, D_MODEL), halo),
        pl.BlockSpec((D_MODEL, D_FF), const),
        pl.BlockSpec((D_MODEL, D_FF), const),
        pl.BlockSpec((CONV_WIDTH, D_FF), const),
        pl.BlockSpec((1, D_FF), const),
        pl.BlockSpec((D_FF, D_MODEL), const),
        pl.BlockSpec((1, D_MODEL), const),
        pl.BlockSpec((1, D_MODEL), const),
    ]
    return pl.pallas_call(
        functools.partial(_ffn_kernel, tm=tm, alpha=alpha),
        grid=(B, ns),
        in_specs=in_specs,
        out_specs=pl.BlockSpec((tm, D_MODEL), row),
        out_shape=jax.ShapeDtypeStruct((T, D_MODEL), F32),
        scratch_shapes=[pltpu.VMEM((tm, D_FF), BF16)],
        compiler_params=_compiler_params(2),
        name="ffn",
    )(h, h, wg, wu, cw, cb, wo, lng, lnb)


def _prep_inproj_weights(w_in, b_in):
    scale = HEAD_DIM ** -0.5
    o = 0
    parts = {}
    for name, n in (("qa", SWA_Q), ("ka", SWA_KV), ("va", SWA_KV), ("qb", FOX_W), ("kb", FOX_W),
                    ("vb", FOX_W), ("f", FOX_HEADS), ("ga", D_MODEL), ("gb", D_MODEL)):
        parts[name] = (w_in[:, o:o + n], b_in[o:o + n])
        o += n
    tok = [parts["ka"], parts["ga"], parts["gb"]]
    wtok = jnp.concatenate([w for w, _ in tok], axis=1).astype(BF16)
    btok = jnp.concatenate([b for _, b in tok])[None, :]
    pad_w = jnp.zeros((D_MODEL, F_ROWS - FOX_HEADS), F32)
    pad_b = jnp.zeros((F_ROWS - FOX_HEADS,), F32)
    fm_w = [parts["qa"][0] * scale, parts["va"][0], parts["qb"][0] * scale, parts["kb"][0],
            parts["vb"][0], parts["f"][0], pad_w]
    fm_b = [parts["qa"][1] * scale, parts["va"][1], parts["qb"][1] * scale, parts["kb"][1],
            parts["vb"][1], parts["f"][1], pad_b]
    wfm = jnp.concatenate(fm_w, axis=1).T.astype(BF16)
    bfm = jnp.broadcast_to(jnp.concatenate(fm_b)[:, None], (wfm.shape[0], LANES))
    return wtok, btok, wfm, bfm


def kernel(x, ln_mix_g, ln_mix_b, w_in, b_in, attn_sinks, w_proj_a, w_proj_b, w_out, ln_ffn_g, ln_ffn_b,
           w_ffn_in, conv_w, conv_b, w_ffn_out):
    B, S, D = x.shape
    assert D == D_MODEL and S % BLOCK == 0
    depth = w_in.shape[0]
    alpha = (2 * depth) ** 0.25
    tm = min(ROW_TILE, S)
    ts = min(SWA_TILE, S)
    tq = min(FOX_Q_TILE, S)
    tk = min(FOX_K_TILE, S)
    assert S % tm == 0 and S % ts == 0 and S % tq == 0 and S % tk == 0 and tk % tq == 0

    h = x.reshape(B * S, D)
    for l in range(depth):
        wtok, btok, wfm, bfm = _prep_inproj_weights(w_in[l], b_in[l])
        ka, ga, gb, qat, vat, qbt, kbt, vbt, ct = _inproj(h, wtok, btok, wfm, bfm, B=B, S=S, tm=tm)
        sink_rows = jnp.broadcast_to(
            attn_sinks[l].reshape(SWA_KV_HEADS, SWA_GROUP, 1), (SWA_KV_HEADS, SWA_GROUP, BLOCK)
        ).reshape(SWA_KV_HEADS, 1, SWA_GROUP * BLOCK)
        yat = _swa(qat, ka, vat, sink_rows, B=B, S=S, ts=ts)
        ybt = _fox(qbt, kbt, vbt, ct, B=B, S=S, tq=tq, tk=tk)
        h = _mix(h, yat, ybt, ga, gb, w_proj_a[l].astype(BF16), w_proj_b[l].astype(BF16),
                 w_out[l].astype(BF16), ln_mix_g[l][None, :], ln_mix_b[l][None, :],
                 B=B, S=S, tm=tm, alpha=alpha)
        h = _ffn(h, w_ffn_in[l][:, :D_FF].astype(BF16), w_ffn_in[l][:, D_FF:].astype(BF16),
                 conv_w[l], conv_b[l][None, :], w_ffn_out[l].astype(BF16),
                 ln_ffn_g[l][None, :], ln_ffn_b[l][None, :], B=B, S=S, tm=tm, alpha=alpha)
    return h.reshape(B, S, D)
```

```python
import functools

import jax
import jax.numpy as jnp
from jax import lax
from jax.experimental import pallas as pl
from jax.experimental.pallas import tpu as pltpu

D_MODEL = 1024
HEAD_DIM = 64
SWA_Q_HEADS = 8
SWA_KV_HEADS = 2
SWA_GROUP = SWA_Q_HEADS // SWA_KV_HEADS
SWA_WINDOW = 128
FOX_HEADS = 8
BLOCK = 128
D_FF = 2816
CONV_WIDTH = 3
LN_EPS = 1e-5
NEG_INF = -1e30
SWA_Q = SWA_Q_HEADS * HEAD_DIM
SWA_KV = SWA_KV_HEADS * HEAD_DIM
FOX_W = FOX_HEADS * HEAD_DIM

LANES = 128
F_ROWS = 16
AUG = 128
V_ROWS = HEAD_DIM + 16
VMEM_LIMIT = 56 * 1024 * 1024

ROW_TILE = 512
SWA_TILE = 512
FOX_Q_TILE = 1024
FOX_K_TILE = 1024
LOG2E = 1.4426950408889634
FOX_BOUND_SLACK = 1.0 + 2.0 ** -6
FOX_BOUND_MARGIN = 2.0
FOX_MIN_DENOM = 2.0 ** -60

F32 = jnp.float32
BF16 = jnp.bfloat16


def _compiler_params(n_axes):
    return pltpu.CompilerParams(dimension_semantics=("arbitrary",) * n_axes,
                                vmem_limit_bytes=VMEM_LIMIT)


def _layer_norm(r, g, b):
    mu = jnp.mean(r, axis=-1, keepdims=True)
    d = r - mu
    var = jnp.mean(d * d, axis=-1, keepdims=True)
    return d * lax.rsqrt(var + LN_EPS) * g + b


def _inproj_kernel(x_ref, wtok_ref, btok_ref, wfm_ref, bfm_ref,
                   ka_ref, ga_ref, gb_ref, qat_ref, vat_ref, qbt_ref, kbt_ref, vbt_ref, ct_ref,
                   carry_ref, *, tm):
    i = pl.program_id(1)
    xb = x_ref[...].astype(BF16)
    zt = jnp.dot(xb, wtok_ref[...], preferred_element_type=F32) + btok_ref[...]
    ka_ref[...] = zt[:, :SWA_KV].astype(BF16)
    ga_ref[...] = zt[:, SWA_KV:SWA_KV + D_MODEL].astype(BF16)
    gb_ref[...] = zt[:, SWA_KV + D_MODEL:].astype(BF16)

    zf = lax.dot_general(wfm_ref[...], xb, (((1,), (1,)), ((), ())), preferred_element_type=F32)
    zf = zf + jnp.concatenate([bfm_ref[...]] * (tm // LANES), axis=1)
    o = 0
    qat_ref[...] = zf[o:o + SWA_Q].astype(BF16); o += SWA_Q
    vat_ref[...] = zf[o:o + SWA_KV].astype(BF16); o += SWA_KV
    qbt_ref[...] = (zf[o:o + FOX_W] * LOG2E).astype(BF16); o += FOX_W
    kbt_ref[...] = zf[o:o + FOX_W].astype(BF16); o += FOX_W
    vbt_ref[...] = zf[o:o + FOX_W].astype(BF16); o += FOX_W

    f = zf[o:o + FOX_HEADS]
    lf = jnp.minimum(f, 0.0) - jnp.log1p(jnp.exp(-jnp.abs(f)))
    lane = lax.broadcasted_iota(jnp.int32, lf.shape, 1)
    sh = 1
    while sh < tm:
        lf = lf + jnp.where(lane >= sh, pltpu.roll(lf, sh, 1), 0.0)
        sh *= 2

    @pl.when(i == 0)
    def _():
        carry_ref[...] = jnp.zeros_like(carry_ref)

    c = lf + carry_ref[:, 0:1]
    ct_ref[...] = c * LOG2E
    carry_ref[...] = jnp.broadcast_to(c[:, tm - 1:tm], carry_ref.shape)


def _inproj(h, wtok, btok, wfm, bfm, *, B, S, tm):
    T = B * S
    ns = S // tm
    n_tok = wtok.shape[1]
    n_fm = wfm.shape[0]
    row = lambda b, i: (b * ns + i, 0)
    fm = lambda b, i: (b, 0, i)
    const = lambda b, i: (0, 0)
    out_shape = (
        jax.ShapeDtypeStruct((T, SWA_KV), BF16),
        jax.ShapeDtypeStruct((T, D_MODEL), BF16),
        jax.ShapeDtypeStruct((T, D_MODEL), BF16),
        jax.ShapeDtypeStruct((B, SWA_Q, S), BF16),
        jax.ShapeDtypeStruct((B, SWA_KV, S), BF16),
        jax.ShapeDtypeStruct((B, FOX_W, S), BF16),
        jax.ShapeDtypeStruct((B, FOX_W, S), BF16),
        jax.ShapeDtypeStruct((B, FOX_W, S), BF16),
        jax.ShapeDtypeStruct((B, FOX_HEADS, S), F32),
    )
    out_specs = (
        pl.BlockSpec((tm, SWA_KV), row),
        pl.BlockSpec((tm, D_MODEL), row),
        pl.BlockSpec((tm, D_MODEL), row),
        pl.BlockSpec((None, SWA_Q, tm), fm),
        pl.BlockSpec((None, SWA_KV, tm), fm),
        pl.BlockSpec((None, FOX_W, tm), fm),
        pl.BlockSpec((None, FOX_W, tm), fm),
        pl.BlockSpec((None, FOX_W, tm), fm),
        pl.BlockSpec((None, FOX_HEADS, tm), fm),
    )
    in_specs = [
        pl.BlockSpec((tm, D_MODEL), row),
        pl.BlockSpec((D_MODEL, n_tok), const),
        pl.BlockSpec((1, n_tok), const),
        pl.BlockSpec((n_fm, D_MODEL), const),
        pl.BlockSpec((n_fm, LANES), const),
    ]
    return pl.pallas_call(
        functools.partial(_inproj_kernel, tm=tm),
        grid=(B, ns),
        in_specs=in_specs,
        out_specs=out_specs,
        out_shape=out_shape,
        scratch_shapes=[pltpu.VMEM((FOX_HEADS, LANES), F32)],
        compiler_params=_compiler_params(2),
        name="inproj",
    )(h, wtok, btok, wfm, bfm)


def _swa_kernel(qat_ref, ka_ref, kap_ref, vat_ref, vap_ref, sink_ref, o_ref, bias_ref, *, ts):
    b = pl.program_id(0)
    i = pl.program_id(1)
    nblk = ts // BLOCK
    gw = SWA_GROUP * BLOCK

    @pl.when((b == 0) & (i == 0))
    def _():
        r = lax.broadcasted_iota(jnp.int32, (2 * BLOCK, BLOCK), 0)
        c = lax.broadcasted_iota(jnp.int32, (2 * BLOCK, BLOCK), 1)
        dist = BLOCK + c - r
        valid = (dist >= 0) & (dist < SWA_WINDOW)
        distf = dist.astype(F32)
        for g in range(SWA_KV_HEADS):
            for hh in range(SWA_GROUP):
                slope = 2.0 ** (-8.0 * (g * SWA_GROUP + hh + 1) / SWA_Q_HEADS)
                bias_ref[g, :, hh * BLOCK:(hh + 1) * BLOCK] = jnp.where(valid, -slope * distf, NEG_INF)

    kext = jnp.concatenate([kap_ref[...], ka_ref[...]], axis=0)
    vext = jnp.concatenate([vap_ref[...], vat_ref[...]], axis=1)
    zeros_q = jnp.zeros((HEAD_DIM, gw), BF16)
    for n in range(nblk):
        k_win = kext[n * BLOCK:(n + 2) * BLOCK]
        for g in range(SWA_KV_HEADS):
            qrow = jnp.concatenate(
                [qat_ref[(g * SWA_GROUP + hh) * HEAD_DIM:(g * SWA_GROUP + hh + 1) * HEAD_DIM,
                         n * BLOCK:(n + 1) * BLOCK] for hh in range(SWA_GROUP)], axis=1)
            q_pad = jnp.concatenate([qrow, zeros_q] if g == 0 else [zeros_q, qrow], axis=0)
            s = jnp.dot(k_win, q_pad, preferred_element_type=F32) + bias_ref[g]
            if n == 0:
                top = jnp.where(i == 0, NEG_INF, s[:BLOCK])
                s = jnp.concatenate([top, s[BLOCK:]], axis=0)
            sink = sink_ref[g]
            m = jnp.maximum(jnp.max(s, axis=0, keepdims=True), sink)
            p = jnp.exp(s - m)
            denom = jnp.sum(p, axis=0, keepdims=True) + jnp.exp(sink - m)
            v_win = vext[g * HEAD_DIM:(g + 1) * HEAD_DIM, n * BLOCK:(n + 2) * BLOCK]
            out = jnp.dot(v_win, p.astype(BF16), preferred_element_type=F32) / denom
            for hh in range(SWA_GROUP):
                h0 = (g * SWA_GROUP + hh) * HEAD_DIM
                o_ref[h0:h0 + HEAD_DIM, n * BLOCK:(n + 1) * BLOCK] = (
                    out[:, hh * BLOCK:(hh + 1) * BLOCK].astype(BF16))


def _swa(qat, ka, vat, sink_rows, *, B, S, ts):
    ns = S // ts
    nblk = ts // BLOCK
    ka3 = ka.reshape(B, S, SWA_KV)
    prev = lambda b, i: jnp.maximum(i * nblk - 1, 0)
    in_specs = [
        pl.BlockSpec((None, SWA_Q, ts), lambda b, i: (b, 0, i)),
        pl.BlockSpec((None, ts, SWA_KV), lambda b, i: (b, i, 0)),
        pl.BlockSpec((None, BLOCK, SWA_KV), lambda b, i: (b, prev(b, i), 0)),
        pl.BlockSpec((None, SWA_KV, ts), lambda b, i: (b, 0, i)),
        pl.BlockSpec((None, SWA_KV, BLOCK), lambda b, i: (b, 0, prev(b, i))),
        pl.BlockSpec((SWA_KV_HEADS, 1, SWA_GROUP * BLOCK), lambda b, i: (0, 0, 0)),
    ]
    return pl.pallas_call(
        functools.partial(_swa_kernel, ts=ts),
        grid=(B, ns),
        in_specs=in_specs,
        out_specs=pl.BlockSpec((None, SWA_Q, ts), lambda b, i: (b, 0, i)),
        out_shape=jax.ShapeDtypeStruct((B, SWA_Q, S), BF16),
        scratch_shapes=[pltpu.VMEM((SWA_KV_HEADS, 2 * BLOCK, SWA_GROUP * BLOCK), F32)],
        compiler_params=_compiler_params(2),
        name="swa",
    )(qat, ka3, ka3, vat, vat, sink_rows)


def _split3(c):
    hi = c.astype(BF16).astype(F32)
    r = c - hi
    mid = r.astype(BF16).astype(F32)
    lo = (r - mid).astype(BF16).astype(F32)
    return hi, mid, lo


def _rows8(a, b, c, n):
    rid = lax.broadcasted_iota(jnp.int32, (8, n), 0)
    return jnp.where(rid == 0, a, jnp.where(rid == 1, b, jnp.where(rid == 2, c, 0.0)))


def _fox_kernel(qt_ref, kt_ref, vt_ref, c_ref, cq_ref, o_ref, kaug_ref, vaug_ref, knorm_ref, *, S, tq, tk):
    i = pl.program_id(2)
    nkt = S // tk
    pad_rows = AUG - HEAD_DIM - 16

    @pl.when(i == 0)
    def _():
        ksq = jnp.zeros((1, tk), F32)
        for t in range(nkt):
            sl = slice(t * tk, (t + 1) * tk)
            hi, mid, lo = _split3(c_ref[:, sl])
            kf = kt_ref[:, sl].astype(F32)
            ksq = jnp.maximum(ksq, jnp.sum(kf * kf, axis=0, keepdims=True))
            x = jnp.concatenate([kf, _rows8(1.0, 1.0, 1.0, tk),
                                 _rows8(-hi, -mid, -lo, tk), jnp.zeros((pad_rows, tk), F32)], axis=0)
            kaug_ref[t] = x.T.astype(BF16)
            rid = lax.broadcasted_iota(jnp.int32, (V_ROWS - HEAD_DIM, tk), 0)
            vaug_ref[t] = jnp.concatenate(
                [vt_ref[:, sl], jnp.where(rid == 0, 1.0, 0.0).astype(BF16)], axis=0)
        knorm_ref[...] = jnp.broadcast_to(jnp.sqrt(jnp.max(ksq, axis=1, keepdims=True)), knorm_ref.shape)

    qf = qt_ref[...].astype(F32)
    qnorm = jnp.sqrt(jnp.sum(qf * qf, axis=0, keepdims=True))
    bound = qnorm * knorm_ref[0:1, 0:1] * FOX_BOUND_SLACK + FOX_BOUND_MARGIN
    hi, mid, lo = _split3(cq_ref[...] - bound)
    q_aug = jnp.concatenate([qf, _rows8(hi, mid, lo, tq), _rows8(1.0, 1.0, 1.0, tq),
                             jnp.zeros((pad_rows, tq), F32)], axis=0).astype(BF16)
    nfull = (i * tq) // tk

    def scores(j, masked):
        s = jnp.dot(kaug_ref[j], q_aug, preferred_element_type=F32)
        if masked:
            kidx = j * tk + lax.broadcasted_iota(jnp.int32, s.shape, 0)
            qidx = i * tq + lax.broadcasted_iota(jnp.int32, s.shape, 1)
            s = jnp.where(kidx <= qidx, s, NEG_INF)
        return s

    def tile(j, acc, masked):
        p = jnp.exp2(scores(j, masked)).astype(BF16)
        return acc + jnp.dot(vaug_ref[j], p, preferred_element_type=F32)

    acc = lax.fori_loop(0, nfull, lambda j, a: tile(j, a, False), jnp.zeros((V_ROWS, tq), F32))
    acc = tile(nfull, acc, True)
    denom = acc[HEAD_DIM:HEAD_DIM + 1]
    o_ref[...] = (acc[:HEAD_DIM] / denom).astype(BF16)

    @pl.when(jnp.logical_not(jnp.min(denom) >= FOX_MIN_DENOM))
    def _():
        def tile_exact(j, carry, masked):
            m, acc = carry
            s = scores(j, masked)
            m_new = jnp.maximum(m, jnp.max(s, axis=0, keepdims=True))
            p = jnp.exp2(s - m_new).astype(BF16)
            acc = acc * jnp.exp2(m - m_new) + jnp.dot(vaug_ref[j], p, preferred_element_type=F32)
            return m_new, acc

        carry = (jnp.full((1, tq), NEG_INF, F32), jnp.zeros((V_ROWS, tq), F32))
        carry = lax.fori_loop(0, nfull, lambda j, cr: tile_exact(j, cr, False), carry)
        _, acc2 = tile_exact(nfull, carry, True)
        o_ref[...] = (acc2[:HEAD_DIM] / acc2[HEAD_DIM:HEAD_DIM + 1]).astype(BF16)


def _fox(qbt, kbt, vbt, ct, *, B, S, tq, tk):
    nq = S // tq
    c3 = ct.reshape(B * FOX_HEADS, 1, S)
    in_specs = [
        pl.BlockSpec((None, HEAD_DIM, tq), lambda b, h, i: (b, h, i)),
        pl.BlockSpec((None, HEAD_DIM, S), lambda b, h, i: (b, h, 0)),
        pl.BlockSpec((None, HEAD_DIM, S), lambda b, h, i: (b, h, 0)),
        pl.BlockSpec((None, 1, S), lambda b, h, i: (b * FOX_HEADS + h, 0, 0)),
        pl.BlockSpec((None, 1, tq), lambda b, h, i: (b * FOX_HEADS + h, 0, i)),
    ]
    return pl.pallas_call(
        functools.partial(_fox_kernel, S=S, tq=tq, tk=tk),
        grid=(B, FOX_HEADS, nq),
        in_specs=in_specs,
        out_specs=pl.BlockSpec((None, HEAD_DIM, tq), lambda b, h, i: (b, h, i)),
        out_shape=jax.ShapeDtypeStruct((B, FOX_W, S), BF16),
        scratch_shapes=[pltpu.VMEM((S // tk, tk, AUG), BF16),
                        pltpu.VMEM((S // tk, V_ROWS, tk), BF16),
                        pltpu.VMEM((8, LANES), F32)],
        compiler_params=_compiler_params(3),
        name="fox",
    )(qbt, kbt, vbt, c3, c3)


def _mix_kernel(h_ref, yat_ref, ybt_ref, ga_ref, gb_ref, wpa_ref, wpb_ref, wout_ref, lng_ref, lnb_ref,
                o_ref, *, alpha):
    tn = (((0,), (0,)), ((), ()))
    ya = lax.dot_general(yat_ref[...], wpa_ref[...], tn, preferred_element_type=F32)
    yb = lax.dot_general(ybt_ref[...], wpb_ref[...], tn, preferred_element_type=F32)
    merged = (jax.nn.sigmoid(ga_ref[...].astype(F32)) * ya
              + jax.nn.sigmoid(gb_ref[...].astype(F32)) * yb)
    mix = jnp.dot(merged.astype(BF16), wout_ref[...], preferred_element_type=F32)
    o_ref[...] = _layer_norm(alpha * h_ref[...] + mix, lng_ref[...], lnb_ref[...])


def _mix(h, yat, ybt, ga, gb, wpa, wpb, wout, lng, lnb, *, B, S, tm, alpha):
    T = B * S
    ns = S // tm
    row = lambda b, i: (b * ns + i, 0)
    fm = lambda b, i: (b, 0, i)
    const = lambda b, i: (0, 0)
    in_specs = [
        pl.BlockSpec((tm, D_MODEL), row),
        pl.BlockSpec((None, SWA_Q, tm), fm),
        pl.BlockSpec((None, FOX_W, tm), fm),
        pl.BlockSpec((tm, D_MODEL), row),
        pl.BlockSpec((tm, D_MODEL), row),
        pl.BlockSpec((SWA_Q, D_MODEL), const),
        pl.BlockSpec((FOX_W, D_MODEL), const),
        pl.BlockSpec((D_MODEL, D_MODEL), const),
        pl.BlockSpec((1, D_MODEL), const),
        pl.BlockSpec((1, D_MODEL), const),
    ]
    return pl.pallas_call(
        functools.partial(_mix_kernel, alpha=alpha),
        grid=(B, ns),
        in_specs=in_specs,
        out_specs=pl.BlockSpec((tm, D_MODEL), row),
        out_shape=jax.ShapeDtypeStruct((T, D_MODEL), F32),
        compiler_params=_compiler_params(2),
        name="mix",
    )(h, yat, ybt, ga, gb, wpa, wpb, wout, lng, lnb)


FFN_HALO = 16
FFN_CHUNK = 256


def _ffn_kernel(h_ref, hp_ref, wg_ref, wu_ref, cw_ref, cb_ref, wo_ref, lng_ref, lnb_ref, o_ref, a_ref,
                *, tm, alpha):
    i = pl.program_id(1)
    h = h_ref[...]
    hb = h.astype(BF16)
    hp = jnp.where(i == 0, 0.0, hp_ref[...]).astype(BF16)
    hx = jnp.concatenate([hp, hb], axis=0)
    for c in range(D_FF // FFN_CHUNK):
        sl = slice(c * FFN_CHUNK, (c + 1) * FFN_CHUNK)
        g = jnp.dot(hx, wg_ref[:, sl], preferred_element_type=F32)
        u = jnp.dot(hb, wu_ref[:, sl], preferred_element_type=F32)
        conv = cb_ref[:, sl] + cw_ref[2:3, sl] * g[FFN_HALO:]
        conv = conv + cw_ref[1:2, sl] * pltpu.roll(g, 1, 0)[FFN_HALO:]
        conv = conv + cw_ref[0:1, sl] * pltpu.roll(g, 2, 0)[FFN_HALO:]
        a_ref[:, sl] = (conv * jax.nn.sigmoid(conv) * u).astype(BF16)
    ffn = jnp.dot(a_ref[...], wo_ref[...], preferred_element_type=F32)
    o_ref[...] = _layer_norm(alpha * h + ffn, lng_ref[...], lnb_ref[...])


def _ffn(h, wg, wu, cw, cb, wo, lng, lnb, *, B, S, tm, alpha):
    T = B * S
    ns = S // tm
    row = lambda b, i: (b * ns + i, 0)
    halo = lambda b, i: (jnp.maximum((b * ns + i) * (tm // FFN_HALO) - 1, 0), 0)
    const = lambda b, i: (0, 0)
    in_specs = [
        pl.BlockSpec((tm, D_MODEL), row),
        pl.BlockSpec((FFN_HALO, D_MODEL), halo),
        pl.BlockSpec((D_MODEL, D_FF), const),
        pl.BlockSpec((D_MODEL, D_FF), const),
        pl.BlockSpec((CONV_WIDTH, D_FF), const),
        pl.BlockSpec((1, D_FF), const),
        pl.BlockSpec((D_FF, D_MODEL), const),
        pl.BlockSpec((1, D_MODEL), const),
        pl.BlockSpec((1, D_MODEL), const),
    ]
    return pl.pallas_call(
        functools.partial(_ffn_kernel, tm=tm, alpha=alpha),
        grid=(B, ns),
        in_specs=in_specs,
        out_specs=pl.BlockSpec((tm, D_MODEL), row),
        out_shape=jax.ShapeDtypeStruct((T, D_MODEL), F32),
        scratch_shapes=[pltpu.VMEM((tm, D_FF), BF16)],
        compiler_params=_compiler_params(2),
        name="ffn",
    )(h, h, wg, wu, cw, cb, wo, lng, lnb)


def _prep_inproj_weights(w_in, b_in):
    scale = HEAD_DIM ** -0.5
    o = 0
    parts = {}
    for name, n in (("qa", SWA_Q), ("ka", SWA_KV), ("va", SWA_KV), ("qb", FOX_W), ("kb", FOX_W),
                    ("vb", FOX_W), ("f", FOX_HEADS), ("ga", D_MODEL), ("gb", D_MODEL)):
        parts[name] = (w_in[:, o:o + n], b_in[o:o + n])
        o += n
    tok = [parts["ka"], parts["ga"], parts["gb"]]
    wtok = jnp.concatenate([w for w, _ in tok], axis=1).astype(BF16)
    btok = jnp.concatenate([b for _, b in tok])[None, :]
    pad_w = jnp.zeros((D_MODEL, F_ROWS - FOX_HEADS), F32)
    pad_b = jnp.zeros((F_ROWS - FOX_HEADS,), F32)
    fm_w = [parts["qa"][0] * scale, parts["va"][0], parts["qb"][0] * scale, parts["kb"][0],
            parts["vb"][0], parts["f"][0], pad_w]
    fm_b = [parts["qa"][1] * scale, parts["va"][1], parts["qb"][1] * scale, parts["kb"][1],
            parts["vb"][1], parts["f"][1], pad_b]
    wfm = jnp.concatenate(fm_w, axis=1).T.astype(BF16)
    bfm = jnp.broadcast_to(jnp.concatenate(fm_b)[:, None], (wfm.shape[0], LANES))
    return wtok, btok, wfm, bfm


def kernel(x, ln_mix_g, ln_mix_b, w_in, b_in, attn_sinks, w_proj_a, w_proj_b, w_out, ln_ffn_g, ln_ffn_b,
           w_ffn_in, conv_w, conv_b, w_ffn_out):
    B, S, D = x.shape
    assert D == D_MODEL and S % BLOCK == 0
    depth = w_in.shape[0]
    alpha = (2 * depth) ** 0.25
    tm = min(ROW_TILE, S)
    ts = min(SWA_TILE, S)
    tq = min(FOX_Q_TILE, S)
    tk = min(FOX_K_TILE, S)
    assert S % tm == 0 and S % ts == 0 and S % tq == 0 and S % tk == 0 and tk % tq == 0

    h = x.reshape(B * S, D)
    for l in range(depth):
        wtok, btok, wfm, bfm = _prep_inproj_weights(w_in[l], b_in[l])
        ka, ga, gb, qat, vat, qbt, kbt, vbt, ct = _inproj(h, wtok, btok, wfm, bfm, B=B, S=S, tm=tm)
        sink_rows = jnp.broadcast_to(
            attn_sinks[l].reshape(SWA_KV_HEADS, SWA_GROUP, 1), (SWA_KV_HEADS, SWA_GROUP, BLOCK)
        ).reshape(SWA_KV_HEADS, 1, SWA_GROUP * BLOCK)
        yat = _swa(qat, ka, vat, sink_rows, B=B, S=S, ts=ts)
        ybt = _fox(qbt, kbt, vbt, ct, B=B, S=S, tq=tq, tk=tk)
        h = _mix(h, yat, ybt, ga, gb, w_proj_a[l].astype(BF16), w_proj_b[l].astype(BF16),
                 w_out[l].astype(BF16), ln_mix_g[l][None, :], ln_mix_b[l][None, :],
                 B=B, S=S, tm=tm, alpha=alpha)
        h = _ffn(h, w_ffn_in[l][:, :D_FF].astype(BF16), w_ffn_in[l][:, D_FF:].astype(BF16),
                 conv_w[l], conv_b[l][None, :], w_ffn_out[l].astype(BF16),
                 ln_ffn_g[l][None, :], ln_ffn_b[l][None, :], B=B, S=S, tm=tm, alpha=alpha)
    return h.reshape(B, S, D)
```

```python
import functools

import jax
import jax.numpy as jnp
from jax import lax
from jax.experimental import pallas as pl
from jax.experimental.pallas import tpu as pltpu

D_MODEL = 1024
HEAD_DIM = 64
SWA_Q_HEADS = 8
SWA_KV_HEADS = 2
SWA_GROUP = SWA_Q_HEADS // SWA_KV_HEADS
SWA_WINDOW = 128
FOX_HEADS = 8
BLOCK = 128
D_FF = 2816
CONV_WIDTH = 3
LN_EPS = 1e-5
NEG_INF = -1e30
SWA_Q = SWA_Q_HEADS * HEAD_DIM
SWA_KV = SWA_KV_HEADS * HEAD_DIM
FOX_W = FOX_HEADS * HEAD_DIM

LANES = 128
F_ROWS = 16
AUG = 128
V_ROWS = HEAD_DIM + 16
VMEM_LIMIT = 56 * 1024 * 1024

ROW_TILE = 512
SWA_TILE = 512
FOX_Q_TILE = 1024
FOX_K_TILE = 1024
FOX_WIDE = 2
FOX_DIAG_PIECE = 256
LN_PIECES = 2
LOG2E = 1.4426950408889634
FOX_BOUND_SLACK = 1.0 + 2.0 ** -6
FOX_BOUND_MARGIN = 2.0
FOX_MIN_DENOM = 2.0 ** -60

F32 = jnp.float32
BF16 = jnp.bfloat16


def _compiler_params(n_axes):
    return pltpu.CompilerParams(dimension_semantics=("arbitrary",) * n_axes,
                                vmem_limit_bytes=VMEM_LIMIT)


def _layer_norm(r, g, b):
    mu = jnp.mean(r, axis=-1, keepdims=True)
    d = r - mu
    var = jnp.mean(d * d, axis=-1, keepdims=True)
    return d * lax.rsqrt(var + LN_EPS) * g + b


def _inproj_kernel(x_ref, wtok_ref, btok_ref, wfm_ref, bfm_ref,
                   ka_ref, ga_ref, gb_ref, qat_ref, vat_ref, qbt_ref, kbt_ref, vbt_ref, ct_ref,
                   carry_ref, *, tm):
    i = pl.program_id(1)

    @pl.when(i == 0)
    def _():
        carry_ref[...] = jnp.zeros_like(carry_ref)

    xb = x_ref[...].astype(BF16)

    def fm_rows(lo, hi):
        z = lax.dot_general(wfm_ref[lo:hi, :], xb, (((1,), (1,)), ((), ())), preferred_element_type=F32)
        return z + jnp.concatenate([bfm_ref[lo:hi, :]] * (tm // LANES), axis=1)

    def tok_cols(lo, hi):
        return jnp.dot(xb, wtok_ref[:, lo:hi], preferred_element_type=F32) + btok_ref[:, lo:hi]

    o_qb = SWA_Q + SWA_KV
    o_kb, o_vb, o_f = o_qb + FOX_W, o_qb + 2 * FOX_W, o_qb + 3 * FOX_W
    z = fm_rows(o_vb, o_f + F_ROWS)
    vbt_ref[...] = z[:FOX_W].astype(BF16)

    f = z[FOX_W:FOX_W + FOX_HEADS]
    lf = jnp.minimum(f, 0.0) - jnp.log1p(jnp.exp(-jnp.abs(f)))
    lane = lax.broadcasted_iota(jnp.int32, lf.shape, 1)
    sh = 1
    while sh < tm:
        lf = lf + jnp.where(lane >= sh, pltpu.roll(lf, sh, 1), 0.0)
        sh *= 2

    c = lf + carry_ref[:, 0:1]
    ct_ref[...] = c * LOG2E
    carry_ref[...] = jnp.broadcast_to(c[:, tm - 1:tm], carry_ref.shape)

    z = fm_rows(0, o_qb)
    qat_ref[...] = (z[:SWA_Q] * LOG2E).astype(BF16)
    vat_ref[...] = z[SWA_Q:].astype(BF16)
    qbt_ref[...] = (fm_rows(o_qb, o_kb) * LOG2E).astype(BF16)
    kbt_ref[...] = fm_rows(o_kb, o_vb).astype(BF16)
    gb_ref[...] = tok_cols(SWA_KV + D_MODEL, SWA_KV + 2 * D_MODEL).astype(BF16)
    ga_ref[...] = tok_cols(SWA_KV, SWA_KV + D_MODEL).astype(BF16)
    ka_ref[...] = tok_cols(0, SWA_KV).astype(BF16)


def _inproj(h, wtok, btok, wfm, bfm, *, B, S, tm):
    T = B * S
    ns = S // tm
    n_tok = wtok.shape[1]
    n_fm = wfm.shape[0]
    row = lambda b, i: (b * ns + i, 0)
    fm = lambda b, i: (b, 0, i)
    const = lambda b, i: (0, 0)
    out_shape = (
        jax.ShapeDtypeStruct((T, SWA_KV), BF16),
        jax.ShapeDtypeStruct((T, D_MODEL), BF16),
        jax.ShapeDtypeStruct((T, D_MODEL), BF16),
        jax.ShapeDtypeStruct((B, SWA_Q, S), BF16),
        jax.ShapeDtypeStruct((B, SWA_KV, S), BF16),
        jax.ShapeDtypeStruct((B, FOX_W, S), BF16),
        jax.ShapeDtypeStruct((B, FOX_W, S), BF16),
        jax.ShapeDtypeStruct((B, FOX_W, S), BF16),
        jax.ShapeDtypeStruct((B, FOX_HEADS, S), F32),
    )
    out_specs = (
        pl.BlockSpec((tm, SWA_KV), row),
        pl.BlockSpec((tm, D_MODEL), row),
        pl.BlockSpec((tm, D_MODEL), row),
        pl.BlockSpec((None, SWA_Q, tm), fm),
        pl.BlockSpec((None, SWA_KV, tm), fm),
        pl.BlockSpec((None, FOX_W, tm), fm),
        pl.BlockSpec((None, FOX_W, tm), fm),
        pl.BlockSpec((None, FOX_W, tm), fm),
        pl.BlockSpec((None, FOX_HEADS, tm), fm),
    )
    in_specs = [
        pl.BlockSpec((tm, D_MODEL), row),
        pl.BlockSpec((D_MODEL, n_tok), const),
        pl.BlockSpec((1, n_tok), const),
        pl.BlockSpec((n_fm, D_MODEL), const),
        pl.BlockSpec((n_fm, LANES), const),
    ]
    return pl.pallas_call(
        functools.partial(_inproj_kernel, tm=tm),
        grid=(B, ns),
        in_specs=in_specs,
        out_specs=out_specs,
        out_shape=out_shape,
        scratch_shapes=[pltpu.VMEM((FOX_HEADS, LANES), F32)],
        compiler_params=_compiler_params(2),
        name="inproj",
    )(h, wtok, btok, wfm, bfm)


def _swa_kernel(qat_ref, ka_ref, kap_ref, vat_ref, vap_ref, sink_ref, o_ref, bias_ref, *, ts):
    b = pl.program_id(0)
    i = pl.program_id(1)
    nblk = ts // BLOCK
    gw = SWA_GROUP * BLOCK

    @pl.when((b == 0) & (i == 0))
    def _():
        r = lax.broadcasted_iota(jnp.int32, (2 * BLOCK, BLOCK), 0)
        c = lax.broadcasted_iota(jnp.int32, (2 * BLOCK, BLOCK), 1)
        dist = BLOCK + c - r
        valid = (dist >= 0) & (dist < SWA_WINDOW)
        distf = dist.astype(F32)
        for g in range(SWA_KV_HEADS):
            for hh in range(SWA_GROUP):
                slope = 2.0 ** (-8.0 * (g * SWA_GROUP + hh + 1) / SWA_Q_HEADS)
                bias_ref[g, :, hh * BLOCK:(hh + 1) * BLOCK] = jnp.where(valid, (-slope * LOG2E) * distf, NEG_INF)

    kext = jnp.concatenate([kap_ref[...], ka_ref[...]], axis=0)
    vext = jnp.concatenate([vap_ref[...], vat_ref[...]], axis=1)
    zeros_q = jnp.zeros((HEAD_DIM, gw), BF16)
    pieces = [(n, g) for n in range(nblk) for g in range(SWA_KV_HEADS)]

    scores = []
    for n, g in pieces:
        qrow = jnp.concatenate(
            [qat_ref[(g * SWA_GROUP + hh) * HEAD_DIM:(g * SWA_GROUP + hh + 1) * HEAD_DIM,
                     n * BLOCK:(n + 1) * BLOCK] for hh in range(SWA_GROUP)], axis=1)
        q_pad = jnp.concatenate([qrow, zeros_q] if g == 0 else [zeros_q, qrow], axis=0)
        k_win = kext[n * BLOCK:(n + 2) * BLOCK]
        scores.append(jnp.dot(k_win, q_pad, preferred_element_type=F32))

    weights = []
    for (n, g), s in zip(pieces, scores):
        s = s + bias_ref[g]
        if n == 0:
            top = jnp.where(i == 0, NEG_INF, s[:BLOCK])
            s = jnp.concatenate([top, s[BLOCK:]], axis=0)
        sink = sink_ref[g] * LOG2E
        m = jnp.maximum(jnp.max(s, axis=0, keepdims=True), sink)
        p = jnp.exp2(s - m)
        denom = jnp.sum(p, axis=0, keepdims=True) + jnp.exp2(sink - m)
        weights.append((p.astype(BF16), denom))

    for (n, g), (p, denom) in zip(pieces, weights):
        v_win = vext[g * HEAD_DIM:(g + 1) * HEAD_DIM, n * BLOCK:(n + 2) * BLOCK]
        out = jnp.dot(v_win, p, preferred_element_type=F32) / denom
        for hh in range(SWA_GROUP):
            h0 = (g * SWA_GROUP + hh) * HEAD_DIM
            o_ref[h0:h0 + HEAD_DIM, n * BLOCK:(n + 1) * BLOCK] = (
                out[:, hh * BLOCK:(hh + 1) * BLOCK].astype(BF16))


def _swa(qat, ka, vat, sink_rows, *, B, S, ts):
    ns = S // ts
    nblk = ts // BLOCK
    ka3 = ka.reshape(B, S, SWA_KV)
    prev = lambda b, i: jnp.maximum(i * nblk - 1, 0)
    in_specs = [
        pl.BlockSpec((None, SWA_Q, ts), lambda b, i: (b, 0, i)),
        pl.BlockSpec((None, ts, SWA_KV), lambda b, i: (b, i, 0)),
        pl.BlockSpec((None, BLOCK, SWA_KV), lambda b, i: (b, prev(b, i), 0)),
        pl.BlockSpec((None, SWA_KV, ts), lambda b, i: (b, 0, i)),
        pl.BlockSpec((None, SWA_KV, BLOCK), lambda b, i: (b, 0, prev(b, i))),
        pl.BlockSpec((SWA_KV_HEADS, 1, SWA_GROUP * BLOCK), lambda b, i: (0, 0, 0)),
    ]
    return pl.pallas_call(
        functools.partial(_swa_kernel, ts=ts),
        grid=(B, ns),
        in_specs=in_specs,
        out_specs=pl.BlockSpec((None, SWA_Q, ts), lambda b, i: (b, 0, i)),
        out_shape=jax.ShapeDtypeStruct((B, SWA_Q, S), BF16),
        scratch_shapes=[pltpu.VMEM((SWA_KV_HEADS, 2 * BLOCK, SWA_GROUP * BLOCK), F32)],
        compiler_params=_compiler_params(2),
        name="swa",
    )(qat, ka3, ka3, vat, vat, sink_rows)


def _split3(c):
    hi = c.astype(BF16).astype(F32)
    r = c - hi
    mid = r.astype(BF16).astype(F32)
    lo = (r - mid).astype(BF16).astype(F32)
    return hi, mid, lo


def _rows8(a, b, c, n):
    rid = lax.broadcasted_iota(jnp.int32, (8, n), 0)
    return jnp.where(rid == 0, a, jnp.where(rid == 1, b, jnp.where(rid == 2, c, 0.0)))


def _fox_kernel(qt_ref, kt_ref, vt_ref, c_ref, cq_ref, o_ref, kaug_ref, vaug_ref, knorm_ref, *, S, tq, tk):
    i = pl.program_id(2)
    nkt = S // tk
    pad_rows = AUG - HEAD_DIM - 16

    @pl.when(i == 0)
    def _():
        ksq = jnp.zeros((1, tk), F32)
        for t in range(nkt):
            sl = slice(t * tk, (t + 1) * tk)
            hi, mid, lo = _split3(c_ref[:, sl])
            kf = kt_ref[:, sl].astype(F32)
            ksq = jnp.maximum(ksq, jnp.sum(kf * kf, axis=0, keepdims=True))
            x = jnp.concatenate([kf, _rows8(1.0, 1.0, 1.0, tk),
                                 _rows8(-hi, -mid, -lo, tk), jnp.zeros((pad_rows, tk), F32)], axis=0)
            kaug_ref[t] = x.T.astype(BF16)
            rid = lax.broadcasted_iota(jnp.int32, (V_ROWS - HEAD_DIM, tk), 0)
            vaug_ref[t] = jnp.concatenate(
                [vt_ref[:, sl], jnp.where(rid == 0, 1.0, 0.0).astype(BF16)], axis=0)
        knorm_ref[...] = jnp.broadcast_to(jnp.sqrt(jnp.max(ksq, axis=1, keepdims=True)), knorm_ref.shape)

    qf = qt_ref[...].astype(F32)
    qnorm = jnp.sqrt(jnp.sum(qf * qf, axis=0, keepdims=True))
    bound = qnorm * knorm_ref[0:1, 0:1] * FOX_BOUND_SLACK + FOX_BOUND_MARGIN
    hi, mid, lo = _split3(cq_ref[...] - bound)
    q_aug = jnp.concatenate([qf, _rows8(hi, mid, lo, tq), _rows8(1.0, 1.0, 1.0, tq),
                             jnp.zeros((pad_rows, tq), F32)], axis=0).astype(BF16)
    nfull = (i * tq) // tk

    def scores(j, masked):
        s = jnp.dot(kaug_ref[j], q_aug, preferred_element_type=F32)
        if masked:
            kidx = j * tk + lax.broadcasted_iota(jnp.int32, s.shape, 0)
            qidx = i * tq + lax.broadcasted_iota(jnp.int32, s.shape, 1)
            s = jnp.where(kidx <= qidx, s, NEG_INF)
        return s

    def tile(j, acc):
        p = jnp.exp2(scores(j, False)).astype(BF16)
        return acc + jnp.dot(vaug_ref[j], p, preferred_element_type=F32)

    def wide_tile(t, acc):
        k = kaug_ref[pl.ds(t * FOX_WIDE, FOX_WIDE)].reshape(FOX_WIDE * tk, AUG)
        p = jnp.exp2(jnp.dot(k, q_aug, preferred_element_type=F32)).astype(BF16)
        for w in range(FOX_WIDE):
            acc = acc + jnp.dot(vaug_ref[t * FOX_WIDE + w], p[w * tk:(w + 1) * tk], preferred_element_type=F32)
        return acc

    nwide = nfull // FOX_WIDE
    acc = lax.fori_loop(0, nwide, wide_tile, jnp.zeros((V_ROWS, tq), F32))
    acc = lax.fori_loop(nwide * FOX_WIDE, nfull, tile, acc)

    starts = range(0, tk, FOX_DIAG_PIECE)
    diag = [jnp.dot(kaug_ref[nfull, c0:c0 + FOX_DIAG_PIECE, :], q_aug[:, c0:], preferred_element_type=F32)
            for c0 in starts]
    weights = []
    for s in diag:
        kidx = lax.broadcasted_iota(jnp.int32, s.shape, 0)
        qidx = lax.broadcasted_iota(jnp.int32, s.shape, 1)
        weights.append(jnp.exp2(jnp.where(kidx <= qidx, s, NEG_INF)).astype(BF16))
    for c0, p in zip(starts, weights):
        pv = jnp.dot(vaug_ref[nfull, :, c0:c0 + FOX_DIAG_PIECE], p, preferred_element_type=F32)
        acc = jnp.concatenate([acc[:, :c0], acc[:, c0:] + pv], axis=1) if c0 else acc + pv
    denom = acc[HEAD_DIM:HEAD_DIM + 1]
    o_ref[...] = (acc[:HEAD_DIM] / denom).astype(BF16)

    @pl.when(jnp.logical_not(jnp.min(denom) >= FOX_MIN_DENOM))
    def _():
        def tile_exact(j, carry, masked):
            m, acc = carry
            s = scores(j, masked)
            m_new = jnp.maximum(m, jnp.max(s, axis=0, keepdims=True))
            p = jnp.exp2(s - m_new).astype(BF16)
            acc = acc * jnp.exp2(m - m_new) + jnp.dot(vaug_ref[j], p, preferred_element_type=F32)
            return m_new, acc

        carry = (jnp.full((1, tq), NEG_INF, F32), jnp.zeros((V_ROWS, tq), F32))
        carry = lax.fori_loop(0, nfull, lambda j, cr: tile_exact(j, cr, False), carry)
        _, acc2 = tile_exact(nfull, carry, True)
        o_ref[...] = (acc2[:HEAD_DIM] / acc2[HEAD_DIM:HEAD_DIM + 1]).astype(BF16)


def _fox(qbt, kbt, vbt, ct, *, B, S, tq, tk):
    nq = S // tq
    c3 = ct.reshape(B * FOX_HEADS, 1, S)
    in_specs = [
        pl.BlockSpec((None, HEAD_DIM, tq), lambda b, h, i: (b, h, i)),
        pl.BlockSpec((None, HEAD_DIM, S), lambda b, h, i: (b, h, 0)),
        pl.BlockSpec((None, HEAD_DIM, S), lambda b, h, i: (b, h, 0)),
        pl.BlockSpec((None, 1, S), lambda b, h, i: (b * FOX_HEADS + h, 0, 0)),
        pl.BlockSpec((None, 1, tq), lambda b, h, i: (b * FOX_HEADS + h, 0, i)),
    ]
    return pl.pallas_call(
        functools.partial(_fox_kernel, S=S, tq=tq, tk=tk),
        grid=(B, FOX_HEADS, nq),
        in_specs=in_specs,
        out_specs=pl.BlockSpec((None, HEAD_DIM, tq), lambda b, h, i: (b, h, i)),
        out_shape=jax.ShapeDtypeStruct((B, FOX_W, S), BF16),
        scratch_shapes=[pltpu.VMEM((S // tk, tk, AUG), BF16),
                        pltpu.VMEM((S // tk, V_ROWS, tk), BF16),
                        pltpu.VMEM((8, LANES), F32)],
        compiler_params=_compiler_params(3),
        name="fox",
    )(qbt, kbt, vbt, c3, c3)


def _mix_kernel(h_ref, yat_ref, ybt_ref, ga_ref, gb_ref, wpa_ref, wpb_ref, wout_ref, lng_ref, lnb_ref,
                o_ref, *, alpha):
    tn = (((0,), (0,)), ((), ()))
    ya = lax.dot_general(yat_ref[...], wpa_ref[...], tn, preferred_element_type=F32)
    yb = lax.dot_general(ybt_ref[...], wpb_ref[...], tn, preferred_element_type=F32)
    merged = (jax.nn.sigmoid(ga_ref[...].astype(F32)) * ya
              + jax.nn.sigmoid(gb_ref[...].astype(F32)) * yb).astype(BF16)
    rows = merged.shape[0] // LN_PIECES
    for r in range(LN_PIECES):
        rs = slice(r * rows, (r + 1) * rows)
        mix = jnp.dot(merged[rs], wout_ref[...], preferred_element_type=F32)
        o_ref[rs, :] = _layer_norm(alpha * h_ref[rs, :] + mix, lng_ref[...], lnb_ref[...])


def _mix(h, yat, ybt, ga, gb, wpa, wpb, wout, lng, lnb, *, B, S, tm, alpha):
    T = B * S
    ns = S // tm
    row = lambda b, i: (b * ns + i, 0)
    fm = lambda b, i: (b, 0, i)
    const = lambda b, i: (0, 0)
    in_specs = [
        pl.BlockSpec((tm, D_MODEL), row),
        pl.BlockSpec((None, SWA_Q, tm), fm),
        pl.BlockSpec((None, FOX_W, tm), fm),
        pl.BlockSpec((tm, D_MODEL), row),
        pl.BlockSpec((tm, D_MODEL), row),
        pl.BlockSpec((SWA_Q, D_MODEL), const),
        pl.BlockSpec((FOX_W, D_MODEL), const),
        pl.BlockSpec((D_MODEL, D_MODEL), const),
        pl.BlockSpec((1, D_MODEL), const),
        pl.BlockSpec((1, D_MODEL), const),
    ]
    return pl.pallas_call(
        functools.partial(_mix_kernel, alpha=alpha),
        grid=(B, ns),
        in_specs=in_specs,
        out_specs=pl.BlockSpec((tm, D_MODEL), row),
        out_shape=jax.ShapeDtypeStruct((T, D_MODEL), F32),
        compiler_params=_compiler_params(2),
        name="mix",
    )(h, yat, ybt, ga, gb, wpa, wpb, wout, lng, lnb)


FFN_HALO = 16
FFN_CHUNK = 256


def _ffn_kernel(h_ref, hp_ref, wg_ref, wu_ref, cw_ref, cb_ref, wo_ref, lng_ref, lnb_ref, o_ref, a_ref,
                *, tm, alpha):
    i = pl.program_id(1)
    h = h_ref[...]
    hb = h.astype(BF16)
    hp = jnp.where(i == 0, 0.0, hp_ref[...]).astype(BF16)
    hx = jnp.concatenate([hp, hb], axis=0)
    for c in range(D_FF // FFN_CHUNK):
        sl = slice(c * FFN_CHUNK, (c + 1) * FFN_CHUNK)
        g = jnp.dot(hx, wg_ref[:, sl], preferred_element_type=F32)
        u = jnp.dot(hb, wu_ref[:, sl], preferred_element_type=F32)
        conv = cb_ref[:, sl] + cw_ref[2:3, sl] * g[FFN_HALO:]
        conv = conv + cw_ref[1:2, sl] * pltpu.roll(g, 1, 0)[FFN_HALO:]
        conv = conv + cw_ref[0:1, sl] * pltpu.roll(g, 2, 0)[FFN_HALO:]
        a_ref[:, sl] = (conv * jax.nn.sigmoid(conv) * u).astype(BF16)
    rows = tm // LN_PIECES
    for r in range(LN_PIECES):
        rs = slice(r * rows, (r + 1) * rows)
        ffn = jnp.dot(a_ref[rs, :], wo_ref[...], preferred_element_type=F32)
        o_ref[rs, :] = _layer_norm(alpha * h[rs] + ffn, lng_ref[...], lnb_ref[...])


def _ffn(h, wg, wu, cw, cb, wo, lng, lnb, *, B, S, tm, alpha):
    T = B * S
    ns = S // tm
    row = lambda b, i: (b * ns + i, 0)
    halo = lambda b, i: (jnp.maximum((b * ns + i) * (tm // FFN_HALO) - 1, 0), 0)
    const = lambda b, i: (0, 0)
    in_specs = [
        pl.BlockSpec((tm, D_MODEL), row),
        pl.BlockSpec((FFN_HALO, D_MODEL), halo),
        pl.BlockSpec((D_MODEL, D_FF), const),
        pl.BlockSpec((D_MODEL, D_FF), const),
        pl.BlockSpec((CONV_WIDTH, D_FF), const),
        pl.BlockSpec((1, D_FF), const),
        pl.BlockSpec((D_FF, D_MODEL), const),
        pl.BlockSpec((1, D_MODEL), const),
        pl.BlockSpec((1, D_MODEL), const),
    ]
    return pl.pallas_call(
        functools.partial(_ffn_kernel, tm=tm, alpha=alpha),
        grid=(B, ns),
        in_specs=in_specs,
        out_specs=pl.BlockSpec((tm, D_MODEL), row),
        out_shape=jax.ShapeDtypeStruct((T, D_MODEL), F32),
        scratch_shapes=[pltpu.VMEM((tm, D_FF), BF16)],
        compiler_params=_compiler_params(2),
        name="ffn",
    )(h, h, wg, wu, cw, cb, wo, lng, lnb)


def _prep_inproj_weights(w_in, b_in):
    scale = HEAD_DIM ** -0.5
    o = 0
    parts = {}
    for name, n in (("qa", SWA_Q), ("ka", SWA_KV), ("va", SWA_KV), ("qb", FOX_W), ("kb", FOX_W),
                    ("vb", FOX_W), ("f", FOX_HEADS), ("ga", D_MODEL), ("gb", D_MODEL)):
        parts[name] = (w_in[:, o:o + n], b_in[o:o + n])
        o += n
    tok = [parts["ka"], parts["ga"], parts["gb"]]
    wtok = jnp.concatenate([w for w, _ in tok], axis=1).astype(BF16)
    btok = jnp.concatenate([b for _, b in tok])[None, :]
    pad_w = jnp.zeros((D_MODEL, F_ROWS - FOX_HEADS), F32)
    pad_b = jnp.zeros((F_ROWS - FOX_HEADS,), F32)
    fm_w = [parts["qa"][0] * scale, parts["va"][0], parts["qb"][0] * scale, parts["kb"][0],
            parts["vb"][0], parts["f"][0], pad_w]
    fm_b = [parts["qa"][1] * scale, parts["va"][1], parts["qb"][1] * scale, parts["kb"][1],
            parts["vb"][1], parts["f"][1], pad_b]
    wfm = jnp.concatenate(fm_w, axis=1).T.astype(BF16)
    bfm = jnp.broadcast_to(jnp.concatenate(fm_b)[:, None], (wfm.shape[0], LANES))
    return wtok, btok, wfm, bfm


def kernel(x, ln_mix_g, ln_mix_b, w_in, b_in, attn_sinks, w_proj_a, w_proj_b, w_out, ln_ffn_g, ln_ffn_b,
           w_ffn_in, conv_w, conv_b, w_ffn_out):
    B, S, D = x.shape
    assert D == D_MODEL and S % BLOCK == 0
    depth = w_in.shape[0]
    alpha = (2 * depth) ** 0.25
    tm = min(ROW_TILE, S)
    ts = min(SWA_TILE, S)
    tq = min(FOX_Q_TILE, S)
    tk = min(FOX_K_TILE, S)
    assert S % tm == 0 and S % ts == 0 and S % tq == 0 and tq == tk

    h = x.reshape(B * S, D)
    for l in range(depth):
        wtok, btok, wfm, bfm = _prep_inproj_weights(w_in[l], b_in[l])
        ka, ga, gb, qat, vat, qbt, kbt, vbt, ct = _inproj(h, wtok, btok, wfm, bfm, B=B, S=S, tm=tm)
        sink_rows = jnp.broadcast_to(
            attn_sinks[l].reshape(SWA_KV_HEADS, SWA_GROUP, 1), (SWA_KV_HEADS, SWA_GROUP, BLOCK)
        ).reshape(SWA_KV_HEADS, 1, SWA_GROUP * BLOCK)
        yat = _swa(qat, ka, vat, sink_rows, B=B, S=S, ts=ts)
        ybt = _fox(qbt, kbt, vbt, ct, B=B, S=S, tq=tq, tk=tk)
        h = _mix(h, yat, ybt, ga, gb, w_proj_a[l].astype(BF16), w_proj_b[l].astype(BF16),
                 w_out[l].astype(BF16), ln_mix_g[l][None, :], ln_mix_b[l][None, :],
                 B=B, S=S, tm=tm, alpha=alpha)
        h = _ffn(h, w_ffn_in[l][:, :D_FF].astype(BF16), w_ffn_in[l][:, D_FF:].astype(BF16),
                 conv_w[l], conv_b[l][None, :], w_ffn_out[l].astype(BF16),
                 ln_ffn_g[l][None, :], ln_ffn_b[l][None, :], B=B, S=S, tm=tm, alpha=alpha)
    return h.reshape(B, S, D)
```

```python
import functools

import jax
import jax.numpy as jnp
from jax import lax
from jax.experimental import pallas as pl
from jax.experimental.pallas import tpu as pltpu

D_MODEL = 1024
HEAD_DIM = 64
SWA_Q_HEADS = 8
SWA_KV_HEADS = 2
SWA_GROUP = SWA_Q_HEADS // SWA_KV_HEADS
SWA_WINDOW = 128
FOX_HEADS = 8
BLOCK = 128
D_FF = 2816
CONV_WIDTH = 3
LN_EPS = 1e-5
NEG_INF = -1e30
SWA_Q = SWA_Q_HEADS * HEAD_DIM
SWA_KV = SWA_KV_HEADS * HEAD_DIM
FOX_W = FOX_HEADS * HEAD_DIM

LANES = 128
F_ROWS = 16
AUG = 128
V_ROWS = HEAD_DIM + 16
VMEM_LIMIT = 56 * 1024 * 1024

ROW_TILE = 1024
SWA_TILE = 512
FOX_Q_TILE = 1024
FOX_K_TILE = 1024
FOX_WIDE = 2
FOX_GROUP = 2
FOX_DIAG_PIECE = 256
LN_PIECES = 4
MIX_PIECES = 4
LOG2E = 1.4426950408889634
FOX_BOUND_SLACK = 1.0 + 2.0 ** -6
FOX_BOUND_MARGIN = 2.0
FOX_MIN_DENOM = 2.0 ** -60

F32 = jnp.float32
BF16 = jnp.bfloat16


def _compiler_params(n_axes):
    return pltpu.CompilerParams(dimension_semantics=("arbitrary",) * n_axes,
                                vmem_limit_bytes=VMEM_LIMIT)


def _layer_norm(r, g, b):
    mu = jnp.mean(r, axis=-1, keepdims=True)
    d = r - mu
    var = jnp.mean(d * d, axis=-1, keepdims=True)
    return d * lax.rsqrt(var + LN_EPS) * g + b


def _inproj_kernel(x_ref, wtok_ref, btok_ref, wfm_ref, bfm_ref,
                   ka_ref, ga_ref, gb_ref, qat_ref, vat_ref, qbt_ref, kbt_ref, vbt_ref, ct_ref,
                   carry_ref, *, tm):
    i = pl.program_id(1)

    @pl.when(i == 0)
    def _():
        carry_ref[...] = jnp.zeros_like(carry_ref)

    xb = x_ref[...].astype(BF16)

    def fm_rows(lo, hi):
        z = lax.dot_general(wfm_ref[lo:hi, :], xb, (((1,), (1,)), ((), ())), preferred_element_type=F32)
        return z + jnp.concatenate([bfm_ref[lo:hi, :]] * (tm // LANES), axis=1)

    def tok_cols(lo, hi):
        return jnp.dot(xb, wtok_ref[:, lo:hi], preferred_element_type=F32) + btok_ref[:, lo:hi]

    o_qb = SWA_Q + SWA_KV
    o_kb, o_vb, o_f = o_qb + FOX_W, o_qb + 2 * FOX_W, o_qb + 3 * FOX_W
    z = fm_rows(o_vb, o_f + F_ROWS)
    vbt_ref[...] = z[:FOX_W].astype(BF16)

    f = z[FOX_W:FOX_W + FOX_HEADS]
    lf = jnp.minimum(f, 0.0) - jnp.log1p(jnp.exp(-jnp.abs(f)))
    lane = lax.broadcasted_iota(jnp.int32, lf.shape, 1)
    sh = 1
    while sh < tm:
        lf = lf + jnp.where(lane >= sh, pltpu.roll(lf, sh, 1), 0.0)
        sh *= 2

    c = lf + carry_ref[:, 0:1]
    ct_ref[...] = c * LOG2E
    carry_ref[...] = jnp.broadcast_to(c[:, tm - 1:tm], carry_ref.shape)

    z = fm_rows(0, o_qb)
    qat_ref[...] = (z[:SWA_Q] * LOG2E).astype(BF16)
    vat_ref[...] = z[SWA_Q:].astype(BF16)
    qbt_ref[...] = (fm_rows(o_qb, o_kb) * LOG2E).astype(BF16)
    kbt_ref[...] = fm_rows(o_kb, o_vb).astype(BF16)
    gb_ref[...] = tok_cols(SWA_KV + D_MODEL, SWA_KV + 2 * D_MODEL).astype(BF16)
    ga_ref[...] = tok_cols(SWA_KV, SWA_KV + D_MODEL).astype(BF16)
    ka_ref[...] = tok_cols(0, SWA_KV).astype(BF16)


def _inproj(h, wtok, btok, wfm, bfm, *, B, S, tm):
    T = B * S
    ns = S // tm
    n_tok = wtok.shape[1]
    n_fm = wfm.shape[0]
    row = lambda b, i: (b * ns + i, 0)
    fm = lambda b, i: (b, 0, i)
    const = lambda b, i: (0, 0)
    out_shape = (
        jax.ShapeDtypeStruct((T, SWA_KV), BF16),
        jax.ShapeDtypeStruct((T, D_MODEL), BF16),
        jax.ShapeDtypeStruct((T, D_MODEL), BF16),
        jax.ShapeDtypeStruct((B, SWA_Q, S), BF16),
        jax.ShapeDtypeStruct((B, SWA_KV, S), BF16),
        jax.ShapeDtypeStruct((B, FOX_W, S), BF16),
        jax.ShapeDtypeStruct((B, FOX_W, S), BF16),
        jax.ShapeDtypeStruct((B, FOX_W, S), BF16),
        jax.ShapeDtypeStruct((B, FOX_HEADS, S), F32),
    )
    out_specs = (
        pl.BlockSpec((tm, SWA_KV), row),
        pl.BlockSpec((tm, D_MODEL), row),
        pl.BlockSpec((tm, D_MODEL), row),
        pl.BlockSpec((None, SWA_Q, tm), fm),
        pl.BlockSpec((None, SWA_KV, tm), fm),
        pl.BlockSpec((None, FOX_W, tm), fm),
        pl.BlockSpec((None, FOX_W, tm), fm),
        pl.BlockSpec((None, FOX_W, tm), fm),
        pl.BlockSpec((None, FOX_HEADS, tm), fm),
    )
    in_specs = [
        pl.BlockSpec((tm, D_MODEL), row),
        pl.BlockSpec((D_MODEL, n_tok), const, pipeline_mode=pl.Buffered(1)),
        pl.BlockSpec((1, n_tok), const),
        pl.BlockSpec((n_fm, D_MODEL), const, pipeline_mode=pl.Buffered(1)),
        pl.BlockSpec((n_fm, LANES), const, pipeline_mode=pl.Buffered(1)),
    ]
    return pl.pallas_call(
        functools.partial(_inproj_kernel, tm=tm),
        grid=(B, ns),
        in_specs=in_specs,
        out_specs=out_specs,
        out_shape=out_shape,
        scratch_shapes=[pltpu.VMEM((FOX_HEADS, LANES), F32)],
        compiler_params=_compiler_params(2),
        name="inproj",
    )(h, wtok, btok, wfm, bfm)


def _swa_kernel(qat_ref, ka_ref, kap_ref, vat_ref, vap_ref, sink_ref, o_ref, bias_ref, *, ts):
    b = pl.program_id(0)
    i = pl.program_id(1)
    nblk = ts // BLOCK
    gw = SWA_GROUP * BLOCK

    @pl.when((b == 0) & (i == 0))
    def _():
        r = lax.broadcasted_iota(jnp.int32, (2 * BLOCK, BLOCK), 0)
        c = lax.broadcasted_iota(jnp.int32, (2 * BLOCK, BLOCK), 1)
        dist = BLOCK + c - r
        valid = (dist >= 0) & (dist < SWA_WINDOW)
        distf = dist.astype(F32)
        for g in range(SWA_KV_HEADS):
            for hh in range(SWA_GROUP):
                slope = 2.0 ** (-8.0 * (g * SWA_GROUP + hh + 1) / SWA_Q_HEADS)
                bias_ref[g, :, hh * BLOCK:(hh + 1) * BLOCK] = jnp.where(valid, (-slope * LOG2E) * distf, NEG_INF)

    kext = jnp.concatenate([kap_ref[...], ka_ref[...]], axis=0)
    vext = jnp.concatenate([vap_ref[...], vat_ref[...]], axis=1)
    zeros_q = jnp.zeros((HEAD_DIM, gw), BF16)
    pieces = [(n, g) for n in range(nblk) for g in range(SWA_KV_HEADS)]

    scores = []
    for n, g in pieces:
        qrow = jnp.concatenate(
            [qat_ref[(g * SWA_GROUP + hh) * HEAD_DIM:(g * SWA_GROUP + hh + 1) * HEAD_DIM,
                     n * BLOCK:(n + 1) * BLOCK] for hh in range(SWA_GROUP)], axis=1)
        q_pad = jnp.concatenate([qrow, zeros_q] if g == 0 else [zeros_q, qrow], axis=0)
        k_win = kext[n * BLOCK:(n + 2) * BLOCK]
        scores.append(jnp.dot(k_win, q_pad, preferred_element_type=F32))

    weights = []
    for (n, g), s in zip(pieces, scores):
        s = s + bias_ref[g]
        if n == 0:
            top = jnp.where(i == 0, NEG_INF, s[:BLOCK])
            s = jnp.concatenate([top, s[BLOCK:]], axis=0)
        sink = sink_ref[g] * LOG2E
        m = jnp.maximum(jnp.max(s, axis=0, keepdims=True), sink)
        p = jnp.exp2(s - m)
        denom = jnp.sum(p, axis=0, keepdims=True) + jnp.exp2(sink - m)
        weights.append((p.astype(BF16), denom))

    for (n, g), (p, denom) in zip(pieces, weights):
        v_win = vext[g * HEAD_DIM:(g + 1) * HEAD_DIM, n * BLOCK:(n + 2) * BLOCK]
        out = jnp.dot(v_win, p, preferred_element_type=F32) / denom
        for hh in range(SWA_GROUP):
            h0 = (g * SWA_GROUP + hh) * HEAD_DIM
            o_ref[h0:h0 + HEAD_DIM, n * BLOCK:(n + 1) * BLOCK] = (
                out[:, hh * BLOCK:(hh + 1) * BLOCK].astype(BF16))


def _swa(qat, ka, vat, sink_rows, *, B, S, ts):
    ns = S // ts
    nblk = ts // BLOCK
    ka3 = ka.reshape(B, S, SWA_KV)
    prev = lambda b, i: jnp.maximum(i * nblk - 1, 0)
    in_specs = [
        pl.BlockSpec((None, SWA_Q, ts), lambda b, i: (b, 0, i)),
        pl.BlockSpec((None, ts, SWA_KV), lambda b, i: (b, i, 0)),
        pl.BlockSpec((None, BLOCK, SWA_KV), lambda b, i: (b, prev(b, i), 0)),
        pl.BlockSpec((None, SWA_KV, ts), lambda b, i: (b, 0, i)),
        pl.BlockSpec((None, SWA_KV, BLOCK), lambda b, i: (b, 0, prev(b, i))),
        pl.BlockSpec((SWA_KV_HEADS, 1, SWA_GROUP * BLOCK), lambda b, i: (0, 0, 0)),
    ]
    return pl.pallas_call(
        functools.partial(_swa_kernel, ts=ts),
        grid=(B, ns),
        in_specs=in_specs,
        out_specs=pl.BlockSpec((None, SWA_Q, ts), lambda b, i: (b, 0, i)),
        out_shape=jax.ShapeDtypeStruct((B, SWA_Q, S), BF16),
        scratch_shapes=[pltpu.VMEM((SWA_KV_HEADS, 2 * BLOCK, SWA_GROUP * BLOCK), F32)],
        compiler_params=_compiler_params(2),
        name="swa",
    )(qat, ka3, ka3, vat, vat, sink_rows)


def _split3(c):
    hi = c.astype(BF16).astype(F32)
    r = c - hi
    mid = r.astype(BF16).astype(F32)
    lo = (r - mid).astype(BF16).astype(F32)
    return hi, mid, lo


def _rows8(a, b, c, n):
    rid = lax.broadcasted_iota(jnp.int32, (8, n), 0)
    return jnp.where(rid == 0, a, jnp.where(rid == 1, b, jnp.where(rid == 2, c, 0.0)))


def _fox_kernel(qt_ref, kt_ref, vt_ref, c_ref, cq_ref, o_ref, kaug_ref, vaug_ref, knorm_ref, acc_ref,
                *, S, tq, tk):
    i = pl.program_id(2)
    nkt = S // tk
    pad_rows = AUG - HEAD_DIM - 16
    heads = range(FOX_GROUP)
    hrows = [slice(hd * HEAD_DIM, (hd + 1) * HEAD_DIM) for hd in heads]

    @pl.when(i == 0)
    def _():
        for hd in heads:
            ksq = jnp.zeros((1, tk), F32)
            for t in range(nkt):
                sl = slice(t * tk, (t + 1) * tk)
                hi, mid, lo = _split3(c_ref[hd:hd + 1, sl])
                kf = kt_ref[hrows[hd], sl].astype(F32)
                ksq = jnp.maximum(ksq, jnp.sum(kf * kf, axis=0, keepdims=True))
                x = jnp.concatenate([kf, _rows8(1.0, 1.0, 1.0, tk),
                                     _rows8(-hi, -mid, -lo, tk), jnp.zeros((pad_rows, tk), F32)], axis=0)
                kaug_ref[hd, t] = x.T.astype(BF16)
                rid = lax.broadcasted_iota(jnp.int32, (V_ROWS - HEAD_DIM, tk), 0)
                vaug_ref[hd, t] = jnp.concatenate(
                    [vt_ref[hrows[hd], sl], jnp.where(rid == 0, 1.0, 0.0).astype(BF16)], axis=0)
            knorm_ref[hd] = jnp.broadcast_to(jnp.sqrt(jnp.max(ksq, axis=1, keepdims=True)), knorm_ref.shape[1:])

    q_aug = []
    for hd in heads:
        qf = qt_ref[hrows[hd], :].astype(F32)
        qnorm = jnp.sqrt(jnp.sum(qf * qf, axis=0, keepdims=True))
        bound = qnorm * knorm_ref[hd, 0:1, 0:1] * FOX_BOUND_SLACK + FOX_BOUND_MARGIN
        hi, mid, lo = _split3(cq_ref[hd:hd + 1, :] - bound)
        q_aug.append(jnp.concatenate([qf, _rows8(hi, mid, lo, tq), _rows8(1.0, 1.0, 1.0, tq),
                                      jnp.zeros((pad_rows, tq), F32)], axis=0).astype(BF16))
    nfull = (i * tq) // tk

    def wide_tile(t, accs):
        s = [jnp.dot(kaug_ref[hd, pl.ds(t * FOX_WIDE, FOX_WIDE)].reshape(FOX_WIDE * tk, AUG), q_aug[hd],
                     preferred_element_type=F32) for hd in heads]
        p = [jnp.exp2(x).astype(BF16) for x in s]
        out = []
        for hd in heads:
            acc = accs[hd]
            for w in range(FOX_WIDE):
                acc = acc + jnp.dot(vaug_ref[hd, t * FOX_WIDE + w], p[hd][w * tk:(w + 1) * tk],
                                    preferred_element_type=F32)
            out.append(acc)
        return tuple(out)

    accs = lax.fori_loop(0, nfull // FOX_WIDE, wide_tile,
                         tuple(jnp.zeros((V_ROWS, tq), F32) for _ in heads))

    def finish(accs, odd):
        starts = range(0, tk, FOX_DIAG_PIECE)
        full = [jnp.dot(kaug_ref[hd, nfull - 1], q_aug[hd], preferred_element_type=F32) if odd else None
                for hd in heads]
        diag = [[jnp.dot(kaug_ref[hd, nfull, c0:c0 + FOX_DIAG_PIECE, :], q_aug[hd][:, c0:],
                         preferred_element_type=F32) for c0 in starts] for hd in heads]
        accs = list(accs)
        if odd:
            for hd in heads:
                accs[hd] = accs[hd] + jnp.dot(vaug_ref[hd, nfull - 1], jnp.exp2(full[hd]).astype(BF16),
                                              preferred_element_type=F32)
        weights = []
        for hd in heads:
            w = []
            for s in diag[hd]:
                kidx = lax.broadcasted_iota(jnp.int32, s.shape, 0)
                qidx = lax.broadcasted_iota(jnp.int32, s.shape, 1)
                w.append(jnp.exp2(jnp.where(kidx <= qidx, s, NEG_INF)).astype(BF16))
            weights.append(w)
        for hd in heads:
            acc = accs[hd]
            for c0, p in zip(starts, weights[hd]):
                pv = jnp.dot(vaug_ref[hd, nfull, :, c0:c0 + FOX_DIAG_PIECE], p, preferred_element_type=F32)
                acc = jnp.concatenate([acc[:, :c0], acc[:, c0:] + pv], axis=1) if c0 else acc + pv
            acc_ref[hd] = acc

    @pl.when(nfull % FOX_WIDE == 0)
    def _():
        finish(accs, False)

    @pl.when(nfull % FOX_WIDE == 1)
    def _():
        finish(accs, True)

    dmin = None
    for hd in heads:
        acc = acc_ref[hd]
        denom = acc[HEAD_DIM:HEAD_DIM + 1]
        o_ref[hrows[hd], :] = (acc[:HEAD_DIM] / denom).astype(BF16)
        dmin = denom if dmin is None else jnp.minimum(dmin, denom)

    @pl.when(jnp.logical_not(jnp.min(dmin) >= FOX_MIN_DENOM))
    def _():
        for hd in heads:
            def tile_exact(j, carry, masked):
                m, acc = carry
                s = jnp.dot(kaug_ref[hd, j], q_aug[hd], preferred_element_type=F32)
                if masked:
                    kidx = j * tk + lax.broadcasted_iota(jnp.int32, s.shape, 0)
                    qidx = i * tq + lax.broadcasted_iota(jnp.int32, s.shape, 1)
                    s = jnp.where(kidx <= qidx, s, NEG_INF)
                m_new = jnp.maximum(m, jnp.max(s, axis=0, keepdims=True))
                p = jnp.exp2(s - m_new).astype(BF16)
                acc = acc * jnp.exp2(m - m_new) + jnp.dot(vaug_ref[hd, j], p, preferred_element_type=F32)
                return m_new, acc

            carry = (jnp.full((1, tq), NEG_INF, F32), jnp.zeros((V_ROWS, tq), F32))
            carry = lax.fori_loop(0, nfull, lambda j, cr: tile_exact(j, cr, False), carry)
            _, acc2 = tile_exact(nfull, carry, True)
            o_ref[hrows[hd], :] = (acc2[:HEAD_DIM] / acc2[HEAD_DIM:HEAD_DIM + 1]).astype(BF16)


def _fox(qbt, kbt, vbt, ct, *, B, S, tq, tk):
    nq = S // tq
    ngrp = FOX_HEADS // FOX_GROUP
    gw = FOX_GROUP * HEAD_DIM
    c3 = ct.reshape(B * ngrp, FOX_GROUP, S)
    in_specs = [
        pl.BlockSpec((None, gw, tq), lambda b, h, i: (b, h, i)),
        pl.BlockSpec((None, gw, S), lambda b, h, i: (b, h, 0)),
        pl.BlockSpec((None, gw, S), lambda b, h, i: (b, h, 0)),
        pl.BlockSpec((None, FOX_GROUP, S), lambda b, h, i: (b * ngrp + h, 0, 0)),
        pl.BlockSpec((None, FOX_GROUP, tq), lambda b, h, i: (b * ngrp + h, 0, i)),
    ]
    return pl.pallas_call(
        functools.partial(_fox_kernel, S=S, tq=tq, tk=tk),
        grid=(B, ngrp, nq),
        in_specs=in_specs,
        out_specs=pl.BlockSpec((None, gw, tq), lambda b, h, i: (b, h, i)),
        out_shape=jax.ShapeDtypeStruct((B, FOX_W, S), BF16),
        scratch_shapes=[pltpu.VMEM((FOX_GROUP, S // tk, tk, AUG), BF16),
                        pltpu.VMEM((FOX_GROUP, S // tk, V_ROWS, tk), BF16),
                        pltpu.VMEM((FOX_GROUP, 8, LANES), F32),
                        pltpu.VMEM((FOX_GROUP, V_ROWS, tq), F32)],
        compiler_params=_compiler_params(3),
        name="fox",
    )(qbt, kbt, vbt, c3, c3)


def _mix_kernel(h_ref, yat_ref, ybt_ref, ga_ref, gb_ref, wpa_ref, wpb_ref, wout_ref, lng_ref, lnb_ref,
                o_ref, *, alpha):
    tn = (((0,), (0,)), ((), ()))
    rows = h_ref.shape[0] // MIX_PIECES
    pieces = [slice(r * rows, (r + 1) * rows) for r in range(MIX_PIECES)]
    proj = [(lax.dot_general(yat_ref[:, rs], wpa_ref[...], tn, preferred_element_type=F32),
             lax.dot_general(ybt_ref[:, rs], wpb_ref[...], tn, preferred_element_type=F32)) for rs in pieces]
    for rs, (ya, yb) in zip(pieces, proj):
        merged = (jax.nn.sigmoid(ga_ref[rs, :].astype(F32)) * ya
                  + jax.nn.sigmoid(gb_ref[rs, :].astype(F32)) * yb).astype(BF16)
        mix = jnp.dot(merged, wout_ref[...], preferred_element_type=F32)
        o_ref[rs, :] = _layer_norm(alpha * h_ref[rs, :] + mix, lng_ref[...], lnb_ref[...])


def _mix(h, yat, ybt, ga, gb, wpa, wpb, wout, lng, lnb, *, B, S, tm, alpha):
    T = B * S
    ns = S // tm
    row = lambda b, i: (b * ns + i, 0)
    fm = lambda b, i: (b, 0, i)
    const = lambda b, i: (0, 0)
    in_specs = [
        pl.BlockSpec((tm, D_MODEL), row),
        pl.BlockSpec((None, SWA_Q, tm), fm),
        pl.BlockSpec((None, FOX_W, tm), fm),
        pl.BlockSpec((tm, D_MODEL), row),
        pl.BlockSpec((tm, D_MODEL), row),
        pl.BlockSpec((SWA_Q, D_MODEL), const),
        pl.BlockSpec((FOX_W, D_MODEL), const),
        pl.BlockSpec((D_MODEL, D_MODEL), const),
        pl.BlockSpec((1, D_MODEL), const),
        pl.BlockSpec((1, D_MODEL), const),
    ]
    return pl.pallas_call(
        functools.partial(_mix_kernel, alpha=alpha),
        grid=(B, ns),
        in_specs=in_specs,
        out_specs=pl.BlockSpec((tm, D_MODEL), row),
        out_shape=jax.ShapeDtypeStruct((T, D_MODEL), F32),
        compiler_params=_compiler_params(2),
        name="mix",
    )(h, yat, ybt, ga, gb, wpa, wpb, wout, lng, lnb)


FFN_HALO = 16
FFN_CHUNK = 256


def _ffn_kernel(h_ref, hp_ref, wg_ref, wu_ref, cw_ref, cb_ref, wo_ref, lng_ref, lnb_ref, o_ref, a_ref,
                *, tm, alpha):
    i = pl.program_id(1)
    h = h_ref[...]
    hb = h.astype(BF16)
    hp = jnp.where(i == 0, 0.0, hp_ref[...]).astype(BF16)
    hx = jnp.concatenate([hp, hb], axis=0)
    for c in range(D_FF // FFN_CHUNK):
        sl = slice(c * FFN_CHUNK, (c + 1) * FFN_CHUNK)
        g = jnp.dot(hx, wg_ref[:, sl], preferred_element_type=F32)
        u = jnp.dot(hb, wu_ref[:, sl], preferred_element_type=F32)
        conv = cb_ref[:, sl] + cw_ref[2:3, sl] * g[FFN_HALO:]
        conv = conv + cw_ref[1:2, sl] * pltpu.roll(g, 1, 0)[FFN_HALO:]
        conv = conv + cw_ref[0:1, sl] * pltpu.roll(g, 2, 0)[FFN_HALO:]
        a_ref[:, sl] = (conv * jax.nn.sigmoid(conv) * u).astype(BF16)
    rows = tm // LN_PIECES
    for r in range(LN_PIECES):
        rs = slice(r * rows, (r + 1) * rows)
        ffn = jnp.dot(a_ref[rs, :], wo_ref[...], preferred_element_type=F32)
        o_ref[rs, :] = _layer_norm(alpha * h[rs] + ffn, lng_ref[...], lnb_ref[...])


def _ffn(h, wg, wu, cw, cb, wo, lng, lnb, *, B, S, tm, alpha):
    T = B * S
    ns = S // tm
    row = lambda b, i: (b * ns + i, 0)
    halo = lambda b, i: (jnp.maximum((b * ns + i) * (tm // FFN_HALO) - 1, 0), 0)
    const = lambda b, i: (0, 0)
    in_specs = [
        pl.BlockSpec((tm, D_MODEL), row),
        pl.BlockSpec((FFN_HALO, D_MODEL), halo),
        pl.BlockSpec((D_MODEL, D_FF), const, pipeline_mode=pl.Buffered(1)),
        pl.BlockSpec((D_MODEL, D_FF), const, pipeline_mode=pl.Buffered(1)),
        pl.BlockSpec((CONV_WIDTH, D_FF), const),
        pl.BlockSpec((1, D_FF), const),
        pl.BlockSpec((D_FF, D_MODEL), const, pipeline_mode=pl.Buffered(1)),
        pl.BlockSpec((1, D_MODEL), const),
        pl.BlockSpec((1, D_MODEL), const),
    ]
    return pl.pallas_call(
        functools.partial(_ffn_kernel, tm=tm, alpha=alpha),
        grid=(B, ns),
        in_specs=in_specs,
        out_specs=pl.BlockSpec((tm, D_MODEL), row),
        out_shape=jax.ShapeDtypeStruct((T, D_MODEL), F32),
        scratch_shapes=[pltpu.VMEM((tm, D_FF), BF16)],
        compiler_params=_compiler_params(2),
        name="ffn",
    )(h, h, wg, wu, cw, cb, wo, lng, lnb)


def _prep_inproj_weights(w_in, b_in):
    scale = HEAD_DIM ** -0.5
    o = 0
    parts = {}
    for name, n in (("qa", SWA_Q), ("ka", SWA_KV), ("va", SWA_KV), ("qb", FOX_W), ("kb", FOX_W),
                    ("vb", FOX_W), ("f", FOX_HEADS), ("ga", D_MODEL), ("gb", D_MODEL)):
        parts[name] = (w_in[:, o:o + n], b_in[o:o + n])
        o += n
    tok = [parts["ka"], parts["ga"], parts["gb"]]
    wtok = jnp.concatenate([w for w, _ in tok], axis=1).astype(BF16)
    btok = jnp.concatenate([b for _, b in tok])[None, :]
    pad_w = jnp.zeros((D_MODEL, F_ROWS - FOX_HEADS), F32)
    pad_b = jnp.zeros((F_ROWS - FOX_HEADS,), F32)
    fm_w = [parts["qa"][0] * scale, parts["va"][0], parts["qb"][0] * scale, parts["kb"][0],
            parts["vb"][0], parts["f"][0], pad_w]
    fm_b = [parts["qa"][1] * scale, parts["va"][1], parts["qb"][1] * scale, parts["kb"][1],
            parts["vb"][1], parts["f"][1], pad_b]
    wfm = jnp.concatenate(fm_w, axis=1).T.astype(BF16)
    bfm = jnp.broadcast_to(jnp.concatenate(fm_b)[:, None], (wfm.shape[0], LANES))
    return wtok, btok, wfm, bfm


def kernel(x, ln_mix_g, ln_mix_b, w_in, b_in, attn_sinks, w_proj_a, w_proj_b, w_out, ln_ffn_g, ln_ffn_b,
           w_ffn_in, conv_w, conv_b, w_ffn_out):
    B, S, D = x.shape
    assert D == D_MODEL and S % BLOCK == 0
    depth = w_in.shape[0]
    alpha = (2 * depth) ** 0.25
    tm = min(ROW_TILE, S)
    ts = min(SWA_TILE, S)
    tq = min(FOX_Q_TILE, S)
    tk = min(FOX_K_TILE, S)
    assert S % tm == 0 and S % ts == 0 and S % tq == 0 and tq == tk
    assert FOX_WIDE == 2 and FOX_HEADS % FOX_GROUP == 0 and tm % LN_PIECES == 0 and tm % MIX_PIECES == 0

    h = x.reshape(B * S, D)
    for l in range(depth):
        wtok, btok, wfm, bfm = _prep_inproj_weights(w_in[l], b_in[l])
        ka, ga, gb, qat, vat, qbt, kbt, vbt, ct = _inproj(h, wtok, btok, wfm, bfm, B=B, S=S, tm=tm)
        sink_rows = jnp.broadcast_to(
            attn_sinks[l].reshape(SWA_KV_HEADS, SWA_GROUP, 1), (SWA_KV_HEADS, SWA_GROUP, BLOCK)
        ).reshape(SWA_KV_HEADS, 1, SWA_GROUP * BLOCK)
        yat = _swa(qat, ka, vat, sink_rows, B=B, S=S, ts=ts)
        ybt = _fox(qbt, kbt, vbt, ct, B=B, S=S, tq=tq, tk=tk)
        h = _mix(h, yat, ybt, ga, gb, w_proj_a[l].astype(BF16), w_proj_b[l].astype(BF16),
                 w_out[l].astype(BF16), ln_mix_g[l][None, :], ln_mix_b[l][None, :],
                 B=B, S=S, tm=tm, alpha=alpha)
        h = _ffn(h, w_ffn_in[l][:, :D_FF].astype(BF16), w_ffn_in[l][:, D_FF:].astype(BF16),
                 conv_w[l], conv_b[l][None, :], w_ffn_out[l].astype(BF16),
                 ln_ffn_g[l][None, :], ln_ffn_b[l][None, :], B=B, S=S, tm=tm, alpha=alpha)
    return h.reshape(B, S, D)
```

```python
import functools

import jax
import jax.numpy as jnp
from jax import lax
from jax.experimental import pallas as pl
from jax.experimental.pallas import tpu as pltpu

D_MODEL = 1024
HEAD_DIM = 64
SWA_Q_HEADS = 8
SWA_KV_HEADS = 2
SWA_GROUP = SWA_Q_HEADS // SWA_KV_HEADS
SWA_WINDOW = 128
FOX_HEADS = 8
BLOCK = 128
D_FF = 2816
CONV_WIDTH = 3
LN_EPS = 1e-5
NEG_INF = -1e30
SWA_Q = SWA_Q_HEADS * HEAD_DIM
SWA_KV = SWA_KV_HEADS * HEAD_DIM
FOX_W = FOX_HEADS * HEAD_DIM

LANES = 128
F_ROWS = 16
AUG = 128
V_ROWS = HEAD_DIM + 16
VMEM_LIMIT = 56 * 1024 * 1024

ROW_TILE = 1024
SWA_TILE = 1024
FOX_Q_TILE = 1024
FOX_K_TILE = 1024
FOX_WIDE = 2
FOX_GROUP = 2
FOX_DIAG_PIECE = 256
LN_PIECES = 4
MIX_PIECES = 4
LOG2E = 1.4426950408889634
FOX_BOUND_SLACK = 1.0 + 2.0 ** -6
FOX_BOUND_MARGIN = 2.0
FOX_MIN_DENOM = 2.0 ** -60

F32 = jnp.float32
BF16 = jnp.bfloat16


def _compiler_params(n_axes):
    return pltpu.CompilerParams(dimension_semantics=("arbitrary",) * n_axes,
                                vmem_limit_bytes=VMEM_LIMIT)


def _layer_norm(r, g, b):
    mu = jnp.mean(r, axis=-1, keepdims=True)
    d = r - mu
    var = jnp.mean(d * d, axis=-1, keepdims=True)
    return d * lax.rsqrt(var + LN_EPS) * g + b


def _inproj_kernel(x_ref, wtok_ref, btok_ref, wfm_ref, bfm_ref,
                   ka_ref, ga_ref, gb_ref, qat_ref, vat_ref, qbt_ref, kbt_ref, vbt_ref, ct_ref,
                   carry_ref, *, tm):
    i = pl.program_id(1)

    @pl.when(i == 0)
    def _():
        carry_ref[...] = jnp.zeros_like(carry_ref)

    xb = x_ref[...].astype(BF16)

    def fm_rows(lo, hi):
        z = lax.dot_general(wfm_ref[lo:hi, :], xb, (((1,), (1,)), ((), ())), preferred_element_type=F32)
        return z + jnp.concatenate([bfm_ref[lo:hi, :]] * (tm // LANES), axis=1)

    def tok_cols(lo, hi):
        return jnp.dot(xb, wtok_ref[:, lo:hi], preferred_element_type=F32) + btok_ref[:, lo:hi]

    o_qb = SWA_Q + SWA_KV
    o_kb, o_vb, o_f = o_qb + FOX_W, o_qb + 2 * FOX_W, o_qb + 3 * FOX_W
    z = fm_rows(o_vb, o_f + F_ROWS)
    vbt_ref[...] = z[:FOX_W].astype(BF16)

    f = z[FOX_W:FOX_W + FOX_HEADS]
    lf = jnp.minimum(f, 0.0) - jnp.log1p(jnp.exp(-jnp.abs(f)))
    lane = lax.broadcasted_iota(jnp.int32, lf.shape, 1)
    sh = 1
    while sh < tm:
        lf = lf + jnp.where(lane >= sh, pltpu.roll(lf, sh, 1), 0.0)
        sh *= 2

    c = lf + carry_ref[:, 0:1]
    ct_ref[...] = c * LOG2E
    carry_ref[...] = jnp.broadcast_to(c[:, tm - 1:tm], carry_ref.shape)

    z = fm_rows(0, o_qb)
    qat_ref[...] = (z[:SWA_Q] * LOG2E).astype(BF16)
    vat_ref[...] = z[SWA_Q:].astype(BF16)
    qbt_ref[...] = (fm_rows(o_qb, o_kb) * LOG2E).astype(BF16)
    kbt_ref[...] = fm_rows(o_kb, o_vb).astype(BF16)
    gb_ref[...] = tok_cols(SWA_KV + D_MODEL, SWA_KV + 2 * D_MODEL).astype(BF16)
    ga_ref[...] = tok_cols(SWA_KV, SWA_KV + D_MODEL).astype(BF16)
    ka_ref[...] = tok_cols(0, SWA_KV).astype(BF16)


def _inproj(h, wtok, btok, wfm, bfm, *, B, S, tm):
    T = B * S
    ns = S // tm
    n_tok = wtok.shape[1]
    n_fm = wfm.shape[0]
    row = lambda b, i: (b * ns + i, 0)
    fm = lambda b, i: (b, 0, i)
    const = lambda b, i: (0, 0)
    out_shape = (
        jax.ShapeDtypeStruct((T, SWA_KV), BF16),
        jax.ShapeDtypeStruct((T, D_MODEL), BF16),
        jax.ShapeDtypeStruct((T, D_MODEL), BF16),
        jax.ShapeDtypeStruct((B, SWA_Q, S), BF16),
        jax.ShapeDtypeStruct((B, SWA_KV, S), BF16),
        jax.ShapeDtypeStruct((B, FOX_W, S), BF16),
        jax.ShapeDtypeStruct((B, FOX_W, S), BF16),
        jax.ShapeDtypeStruct((B, FOX_W, S), BF16),
        jax.ShapeDtypeStruct((B, FOX_HEADS, S), F32),
    )
    out_specs = (
        pl.BlockSpec((tm, SWA_KV), row),
        pl.BlockSpec((tm, D_MODEL), row),
        pl.BlockSpec((tm, D_MODEL), row),
        pl.BlockSpec((None, SWA_Q, tm), fm),
        pl.BlockSpec((None, SWA_KV, tm), fm),
        pl.BlockSpec((None, FOX_W, tm), fm),
        pl.BlockSpec((None, FOX_W, tm), fm),
        pl.BlockSpec((None, FOX_W, tm), fm),
        pl.BlockSpec((None, FOX_HEADS, tm), fm),
    )
    in_specs = [
        pl.BlockSpec((tm, D_MODEL), row),
        pl.BlockSpec((D_MODEL, n_tok), const, pipeline_mode=pl.Buffered(1)),
        pl.BlockSpec((1, n_tok), const),
        pl.BlockSpec((n_fm, D_MODEL), const, pipeline_mode=pl.Buffered(1)),
        pl.BlockSpec((n_fm, LANES), const, pipeline_mode=pl.Buffered(1)),
    ]
    return pl.pallas_call(
        functools.partial(_inproj_kernel, tm=tm),
        grid=(B, ns),
        in_specs=in_specs,
        out_specs=out_specs,
        out_shape=out_shape,
        scratch_shapes=[pltpu.VMEM((FOX_HEADS, LANES), F32)],
        compiler_params=_compiler_params(2),
        name="inproj",
    )(h, wtok, btok, wfm, bfm)


def _swa_kernel(qat_ref, ka_ref, kap_ref, vat_ref, vap_ref, sink_ref, o_ref, bias_ref, *, ts):
    b = pl.program_id(0)
    i = pl.program_id(1)
    nblk = ts // BLOCK
    gw = SWA_GROUP * BLOCK

    @pl.when((b == 0) & (i == 0))
    def _():
        r = lax.broadcasted_iota(jnp.int32, (2 * BLOCK, BLOCK), 0)
        c = lax.broadcasted_iota(jnp.int32, (2 * BLOCK, BLOCK), 1)
        dist = BLOCK + c - r
        valid = (dist >= 0) & (dist < SWA_WINDOW)
        distf = dist.astype(F32)
        for g in range(SWA_KV_HEADS):
            for hh in range(SWA_GROUP):
                slope = 2.0 ** (-8.0 * (g * SWA_GROUP + hh + 1) / SWA_Q_HEADS)
                bias_ref[g, :, hh * BLOCK:(hh + 1) * BLOCK] = jnp.where(valid, (-slope * LOG2E) * distf, NEG_INF)

    kext = jnp.concatenate([kap_ref[...], ka_ref[...]], axis=0)
    vext = jnp.concatenate([vap_ref[...], vat_ref[...]], axis=1)
    zeros_q = jnp.zeros((HEAD_DIM, gw), BF16)
    pieces = [(n, g) for n in range(nblk) for g in range(SWA_KV_HEADS)]

    scores = []
    for n, g in pieces:
        qrow = jnp.concatenate(
            [qat_ref[(g * SWA_GROUP + hh) * HEAD_DIM:(g * SWA_GROUP + hh + 1) * HEAD_DIM,
                     n * BLOCK:(n + 1) * BLOCK] for hh in range(SWA_GROUP)], axis=1)
        q_pad = jnp.concatenate([qrow, zeros_q] if g == 0 else [zeros_q, qrow], axis=0)
        k_win = kext[n * BLOCK:(n + 2) * BLOCK]
        scores.append(jnp.dot(k_win, q_pad, preferred_element_type=F32))

    weights = []
    for (n, g), s in zip(pieces, scores):
        s = s + bias_ref[g]
        if n == 0:
            top = jnp.where(i == 0, NEG_INF, s[:BLOCK])
            s = jnp.concatenate([top, s[BLOCK:]], axis=0)
        sink = sink_ref[g] * LOG2E
        m = jnp.maximum(jnp.max(s, axis=0, keepdims=True), sink)
        p = jnp.exp2(s - m)
        denom = jnp.sum(p, axis=0, keepdims=True) + jnp.exp2(sink - m)
        weights.append((p.astype(BF16), denom))

    for (n, g), (p, denom) in zip(pieces, weights):
        v_win = vext[g * HEAD_DIM:(g + 1) * HEAD_DIM, n * BLOCK:(n + 2) * BLOCK]
        out = jnp.dot(v_win, p, preferred_element_type=F32) / denom
        for hh in range(SWA_GROUP):
            h0 = (g * SWA_GROUP + hh) * HEAD_DIM
            o_ref[h0:h0 + HEAD_DIM, n * BLOCK:(n + 1) * BLOCK] = (
                out[:, hh * BLOCK:(hh + 1) * BLOCK].astype(BF16))


def _swa(qat, ka, vat, sink_rows, *, B, S, ts):
    ns = S // ts
    nblk = ts // BLOCK
    ka3 = ka.reshape(B, S, SWA_KV)
    prev = lambda b, i: jnp.maximum(i * nblk - 1, 0)
    in_specs = [
        pl.BlockSpec((None, SWA_Q, ts), lambda b, i: (b, 0, i)),
        pl.BlockSpec((None, ts, SWA_KV), lambda b, i: (b, i, 0)),
        pl.BlockSpec((None, BLOCK, SWA_KV), lambda b, i: (b, prev(b, i), 0)),
        pl.BlockSpec((None, SWA_KV, ts), lambda b, i: (b, 0, i)),
        pl.BlockSpec((None, SWA_KV, BLOCK), lambda b, i: (b, 0, prev(b, i))),
        pl.BlockSpec((SWA_KV_HEADS, 1, SWA_GROUP * BLOCK), lambda b, i: (0, 0, 0)),
    ]
    return pl.pallas_call(
        functools.partial(_swa_kernel, ts=ts),
        grid=(B, ns),
        in_specs=in_specs,
        out_specs=pl.BlockSpec((None, SWA_Q, ts), lambda b, i: (b, 0, i)),
        out_shape=jax.ShapeDtypeStruct((B, SWA_Q, S), BF16),
        scratch_shapes=[pltpu.VMEM((SWA_KV_HEADS, 2 * BLOCK, SWA_GROUP * BLOCK), F32)],
        compiler_params=_compiler_params(2),
        name="swa",
    )(qat, ka3, ka3, vat, vat, sink_rows)


def _split3(c):
    hi = c.astype(BF16).astype(F32)
    r = c - hi
    mid = r.astype(BF16).astype(F32)
    lo = (r - mid).astype(BF16).astype(F32)
    return hi, mid, lo


def _rows8(a, b, c, n):
    rid = lax.broadcasted_iota(jnp.int32, (8, n), 0)
    return jnp.where(rid == 0, a, jnp.where(rid == 1, b, jnp.where(rid == 2, c, 0.0)))


def _fox_kernel(qt_ref, kt_ref, vt_ref, c_ref, cq_ref, o_ref, kaug_ref, vaug_ref, knorm_ref, acc_ref,
                *, S, tq, tk):
    i = pl.program_id(2)
    nkt = S // tk
    pad_rows = AUG - HEAD_DIM - 16
    heads = range(FOX_GROUP)
    hrows = [slice(hd * HEAD_DIM, (hd + 1) * HEAD_DIM) for hd in heads]

    @pl.when(i == 0)
    def _():
        for hd in heads:
            ksq = jnp.zeros((1, tk), F32)
            for t in range(nkt):
                sl = slice(t * tk, (t + 1) * tk)
                hi, mid, lo = _split3(c_ref[hd:hd + 1, sl])
                kf = kt_ref[hrows[hd], sl].astype(F32)
                ksq = jnp.maximum(ksq, jnp.sum(kf * kf, axis=0, keepdims=True))
                x = jnp.concatenate([kf, _rows8(1.0, 1.0, 1.0, tk),
                                     _rows8(-hi, -mid, -lo, tk), jnp.zeros((pad_rows, tk), F32)], axis=0)
                kaug_ref[hd, t] = x.T.astype(BF16)
                rid = lax.broadcasted_iota(jnp.int32, (V_ROWS - HEAD_DIM, tk), 0)
                vaug_ref[hd, t] = jnp.concatenate(
                    [vt_ref[hrows[hd], sl], jnp.where(rid == 0, 1.0, 0.0).astype(BF16)], axis=0)
            knorm_ref[hd] = jnp.broadcast_to(jnp.sqrt(jnp.max(ksq, axis=1, keepdims=True)), knorm_ref.shape[1:])

    q_aug = []
    for hd in heads:
        qf = qt_ref[hrows[hd], :].astype(F32)
        qnorm = jnp.sqrt(jnp.sum(qf * qf, axis=0, keepdims=True))
        bound = qnorm * knorm_ref[hd, 0:1, 0:1] * FOX_BOUND_SLACK + FOX_BOUND_MARGIN
        hi, mid, lo = _split3(cq_ref[hd:hd + 1, :] - bound)
        q_aug.append(jnp.concatenate([qf, _rows8(hi, mid, lo, tq), _rows8(1.0, 1.0, 1.0, tq),
                                      jnp.zeros((pad_rows, tq), F32)], axis=0).astype(BF16))
    nfull = (i * tq) // tk

    def wide_tile(t, accs):
        s = [jnp.dot(kaug_ref[hd, pl.ds(t * FOX_WIDE, FOX_WIDE)].reshape(FOX_WIDE * tk, AUG), q_aug[hd],
                     preferred_element_type=F32) for hd in heads]
        p = [jnp.exp2(x).astype(BF16) for x in s]
        out = []
        for hd in heads:
            acc = accs[hd]
            for w in range(FOX_WIDE):
                acc = acc + jnp.dot(vaug_ref[hd, t * FOX_WIDE + w], p[hd][w * tk:(w + 1) * tk],
                                    preferred_element_type=F32)
            out.append(acc)
        return tuple(out)

    accs = lax.fori_loop(0, nfull // FOX_WIDE, wide_tile,
                         tuple(jnp.zeros((V_ROWS, tq), F32) for _ in heads))

    def finish(accs, odd):
        starts = range(0, tk, FOX_DIAG_PIECE)
        full = [jnp.dot(kaug_ref[hd, nfull - 1], q_aug[hd], preferred_element_type=F32) if odd else None
                for hd in heads]
        diag = [[jnp.dot(kaug_ref[hd, nfull, c0:c0 + FOX_DIAG_PIECE, :], q_aug[hd][:, c0:],
                         preferred_element_type=F32) for c0 in starts] for hd in heads]
        accs = list(accs)
        if odd:
            for hd in heads:
                accs[hd] = accs[hd] + jnp.dot(vaug_ref[hd, nfull - 1], jnp.exp2(full[hd]).astype(BF16),
                                              preferred_element_type=F32)
        weights = []
        for hd in heads:
            w = []
            for s in diag[hd]:
                kidx = lax.broadcasted_iota(jnp.int32, s.shape, 0)
                qidx = lax.broadcasted_iota(jnp.int32, s.shape, 1)
                w.append(jnp.exp2(jnp.where(kidx <= qidx, s, NEG_INF)).astype(BF16))
            weights.append(w)
        for hd in heads:
            acc = accs[hd]
            for c0, p in zip(starts, weights[hd]):
                pv = jnp.dot(vaug_ref[hd, nfull, :, c0:c0 + FOX_DIAG_PIECE], p, preferred_element_type=F32)
                acc = jnp.concatenate([acc[:, :c0], acc[:, c0:] + pv], axis=1) if c0 else acc + pv
            acc_ref[hd] = acc

    @pl.when(nfull % FOX_WIDE == 0)
    def _():
        finish(accs, False)

    @pl.when(nfull % FOX_WIDE == 1)
    def _():
        finish(accs, True)

    dmin = None
    for hd in heads:
        acc = acc_ref[hd]
        denom = acc[HEAD_DIM:HEAD_DIM + 1]
        o_ref[hrows[hd], :] = (acc[:HEAD_DIM] / denom).astype(BF16)
        dmin = denom if dmin is None else jnp.minimum(dmin, denom)

    @pl.when(jnp.logical_not(jnp.min(dmin) >= FOX_MIN_DENOM))
    def _():
        for hd in heads:
            def tile_exact(j, carry, masked):
                m, acc = carry
                s = jnp.dot(kaug_ref[hd, j], q_aug[hd], preferred_element_type=F32)
                if masked:
                    kidx = j * tk + lax.broadcasted_iota(jnp.int32, s.shape, 0)
                    qidx = i * tq + lax.broadcasted_iota(jnp.int32, s.shape, 1)
                    s = jnp.where(kidx <= qidx, s, NEG_INF)
                m_new = jnp.maximum(m, jnp.max(s, axis=0, keepdims=True))
                p = jnp.exp2(s - m_new).astype(BF16)
                acc = acc * jnp.exp2(m - m_new) + jnp.dot(vaug_ref[hd, j], p, preferred_element_type=F32)
                return m_new, acc

            carry = (jnp.full((1, tq), NEG_INF, F32), jnp.zeros((V_ROWS, tq), F32))
            carry = lax.fori_loop(0, nfull, lambda j, cr: tile_exact(j, cr, False), carry)
            _, acc2 = tile_exact(nfull, carry, True)
            o_ref[hrows[hd], :] = (acc2[:HEAD_DIM] / acc2[HEAD_DIM:HEAD_DIM + 1]).astype(BF16)


def _fox(qbt, kbt, vbt, ct, *, B, S, tq, tk):
    nq = S // tq
    ngrp = FOX_HEADS // FOX_GROUP
    gw = FOX_GROUP * HEAD_DIM
    c3 = ct.reshape(B * ngrp, FOX_GROUP, S)
    in_specs = [
        pl.BlockSpec((None, gw, tq), lambda b, h, i: (b, h, i)),
        pl.BlockSpec((None, gw, S), lambda b, h, i: (b, h, 0)),
        pl.BlockSpec((None, gw, S), lambda b, h, i: (b, h, 0)),
        pl.BlockSpec((None, FOX_GROUP, S), lambda b, h, i: (b * ngrp + h, 0, 0)),
        pl.BlockSpec((None, FOX_GROUP, tq), lambda b, h, i: (b * ngrp + h, 0, i)),
    ]
    return pl.pallas_call(
        functools.partial(_fox_kernel, S=S, tq=tq, tk=tk),
        grid=(B, ngrp, nq),
        in_specs=in_specs,
        out_specs=pl.BlockSpec((None, gw, tq), lambda b, h, i: (b, h, i)),
        out_shape=jax.ShapeDtypeStruct((B, FOX_W, S), BF16),
        scratch_shapes=[pltpu.VMEM((FOX_GROUP, S // tk, tk, AUG), BF16),
                        pltpu.VMEM((FOX_GROUP, S // tk, V_ROWS, tk), BF16),
                        pltpu.VMEM((FOX_GROUP, 8, LANES), F32),
                        pltpu.VMEM((FOX_GROUP, V_ROWS, tq), F32)],
        compiler_params=_compiler_params(3),
        name="fox",
    )(qbt, kbt, vbt, c3, c3)


def _mix_kernel(h_ref, yat_ref, ybt_ref, ga_ref, gb_ref, wpa_ref, wpb_ref, wout_ref, lng_ref, lnb_ref,
                o_ref, *, alpha):
    tn = (((0,), (0,)), ((), ()))
    rows = h_ref.shape[0] // MIX_PIECES
    pieces = [slice(r * rows, (r + 1) * rows) for r in range(MIX_PIECES)]
    proj = [(lax.dot_general(yat_ref[:, rs], wpa_ref[...], tn, preferred_element_type=F32),
             lax.dot_general(ybt_ref[:, rs], wpb_ref[...], tn, preferred_element_type=F32)) for rs in pieces]
    for rs, (ya, yb) in zip(pieces, proj):
        merged = (jax.nn.sigmoid(ga_ref[rs, :].astype(F32)) * ya
                  + jax.nn.sigmoid(gb_ref[rs, :].astype(F32)) * yb).astype(BF16)
        mix = jnp.dot(merged, wout_ref[...], preferred_element_type=F32)
        o_ref[rs, :] = _layer_norm(alpha * h_ref[rs, :] + mix, lng_ref[...], lnb_ref[...])


def _mix(h, yat, ybt, ga, gb, wpa, wpb, wout, lng, lnb, *, B, S, tm, alpha):
    T = B * S
    ns = S // tm
    row = lambda b, i: (b * ns + i, 0)
    fm = lambda b, i: (b, 0, i)
    const = lambda b, i: (0, 0)
    in_specs = [
        pl.BlockSpec((tm, D_MODEL), row),
        pl.BlockSpec((None, SWA_Q, tm), fm),
        pl.BlockSpec((None, FOX_W, tm), fm),
        pl.BlockSpec((tm, D_MODEL), row),
        pl.BlockSpec((tm, D_MODEL), row),
        pl.BlockSpec((SWA_Q, D_MODEL), const),
        pl.BlockSpec((FOX_W, D_MODEL), const),
        pl.BlockSpec((D_MODEL, D_MODEL), const),
        pl.BlockSpec((1, D_MODEL), const),
        pl.BlockSpec((1, D_MODEL), const),
    ]
    return pl.pallas_call(
        functools.partial(_mix_kernel, alpha=alpha),
        grid=(B, ns),
        in_specs=in_specs,
        out_specs=pl.BlockSpec((tm, D_MODEL), row),
        out_shape=jax.ShapeDtypeStruct((T, D_MODEL), F32),
        compiler_params=_compiler_params(2),
        name="mix",
    )(h, yat, ybt, ga, gb, wpa, wpb, wout, lng, lnb)


FFN_HALO = 16
FFN_CHUNK = 256


def _ffn_kernel(h_ref, hp_ref, wg_ref, wu_ref, cw_ref, cb_ref, wo_ref, lng_ref, lnb_ref, o_ref, a_ref,
                *, tm, alpha):
    i = pl.program_id(1)
    h = h_ref[...]
    hb = h.astype(BF16)
    hp = jnp.where(i == 0, 0.0, hp_ref[...]).astype(BF16)
    hx = jnp.concatenate([hp, hb], axis=0)
    for c in range(D_FF // FFN_CHUNK):
        sl = slice(c * FFN_CHUNK, (c + 1) * FFN_CHUNK)
        g = jnp.dot(hx, wg_ref[:, sl], preferred_element_type=F32)
        u = jnp.dot(hb, wu_ref[:, sl], preferred_element_type=F32)
        conv = cb_ref[:, sl] + cw_ref[2:3, sl] * g[FFN_HALO:]
        conv = conv + cw_ref[1:2, sl] * pltpu.roll(g, 1, 0)[FFN_HALO:]
        conv = conv + cw_ref[0:1, sl] * pltpu.roll(g, 2, 0)[FFN_HALO:]
        a_ref[:, sl] = (conv * jax.nn.sigmoid(conv) * u).astype(BF16)
    rows = tm // LN_PIECES
    for r in range(LN_PIECES):
        rs = slice(r * rows, (r + 1) * rows)
        ffn = jnp.dot(a_ref[rs, :], wo_ref[...], preferred_element_type=F32)
        o_ref[rs, :] = _layer_norm(alpha * h[rs] + ffn, lng_ref[...], lnb_ref[...])


def _ffn(h, wg, wu, cw, cb, wo, lng, lnb, *, B, S, tm, alpha):
    T = B * S
    ns = S // tm
    row = lambda b, i: (b * ns + i, 0)
    halo = lambda b, i: (jnp.maximum((b * ns + i) * (tm // FFN_HALO) - 1, 0), 0)
    const = lambda b, i: (0, 0)
    in_specs = [
        pl.BlockSpec((tm, D_MODEL), row),
        pl.BlockSpec((FFN_HALO, D_MODEL), halo),
        pl.BlockSpec((D_MODEL, D_FF), const, pipeline_mode=pl.Buffered(1)),
        pl.BlockSpec((D_MODEL, D_FF), const, pipeline_mode=pl.Buffered(1)),
        pl.BlockSpec((CONV_WIDTH, D_FF), const),
        pl.BlockSpec((1, D_FF), const),
        pl.BlockSpec((D_FF, D_MODEL), const, pipeline_mode=pl.Buffered(1)),
        pl.BlockSpec((1, D_MODEL), const),
        pl.BlockSpec((1, D_MODEL), const),
    ]
    return pl.pallas_call(
        functools.partial(_ffn_kernel, tm=tm, alpha=alpha),
        grid=(B, ns),
        in_specs=in_specs,
        out_specs=pl.BlockSpec((tm, D_MODEL), row),
        out_shape=jax.ShapeDtypeStruct((T, D_MODEL), F32),
        scratch_shapes=[pltpu.VMEM((tm, D_FF), BF16)],
        compiler_params=_compiler_params(2),
        name="ffn",
    )(h, h, wg, wu, cw, cb, wo, lng, lnb)


def _prep_inproj_weights(w_in, b_in):
    scale = HEAD_DIM ** -0.5
    o = 0
    parts = {}
    for name, n in (("qa", SWA_Q), ("ka", SWA_KV), ("va", SWA_KV), ("qb", FOX_W), ("kb", FOX_W),
                    ("vb", FOX_W), ("f", FOX_HEADS), ("ga", D_MODEL), ("gb", D_MODEL)):
        parts[name] = (w_in[:, o:o + n], b_in[o:o + n])
        o += n
    tok = [parts["ka"], parts["ga"], parts["gb"]]
    wtok = jnp.concatenate([w for w, _ in tok], axis=1).astype(BF16)
    btok = jnp.concatenate([b for _, b in tok])[None, :]
    pad_w = jnp.zeros((D_MODEL, F_ROWS - FOX_HEADS), F32)
    pad_b = jnp.zeros((F_ROWS - FOX_HEADS,), F32)
    fm_w = [parts["qa"][0] * scale, parts["va"][0], parts["qb"][0] * scale, parts["kb"][0],
            parts["vb"][0], parts["f"][0], pad_w]
    fm_b = [parts["qa"][1] * scale, parts["va"][1], parts["qb"][1] * scale, parts["kb"][1],
            parts["vb"][1], parts["f"][1], pad_b]
    wfm = jnp.concatenate(fm_w, axis=1).T.astype(BF16)
    bfm = jnp.broadcast_to(jnp.concatenate(fm_b)[:, None], (wfm.shape[0], LANES))
    return wtok, btok, wfm, bfm


def kernel(x, ln_mix_g, ln_mix_b, w_in, b_in, attn_sinks, w_proj_a, w_proj_b, w_out, ln_ffn_g, ln_ffn_b,
           w_ffn_in, conv_w, conv_b, w_ffn_out):
    B, S, D = x.shape
    assert D == D_MODEL and S % BLOCK == 0
    depth = w_in.shape[0]
    alpha = (2 * depth) ** 0.25
    tm = min(ROW_TILE, S)
    ts = min(SWA_TILE, S)
    tq = min(FOX_Q_TILE, S)
    tk = min(FOX_K_TILE, S)
    assert S % tm == 0 and S % ts == 0 and S % tq == 0 and tq == tk
    assert FOX_WIDE == 2 and FOX_HEADS % FOX_GROUP == 0 and tm % LN_PIECES == 0 and tm % MIX_PIECES == 0

    h = x.reshape(B * S, D)
    for l in range(depth):
        wtok, btok, wfm, bfm = _prep_inproj_weights(w_in[l], b_in[l])
        ka, ga, gb, qat, vat, qbt, kbt, vbt, ct = _inproj(h, wtok, btok, wfm, bfm, B=B, S=S, tm=tm)
        sink_rows = jnp.broadcast_to(
            attn_sinks[l].reshape(SWA_KV_HEADS, SWA_GROUP, 1), (SWA_KV_HEADS, SWA_GROUP, BLOCK)
        ).reshape(SWA_KV_HEADS, 1, SWA_GROUP * BLOCK)
        yat = _swa(qat, ka, vat, sink_rows, B=B, S=S, ts=ts)
        ybt = _fox(qbt, kbt, vbt, ct, B=B, S=S, tq=tq, tk=tk)
        h = _mix(h, yat, ybt, ga, gb, w_proj_a[l].astype(BF16), w_proj_b[l].astype(BF16),
                 w_out[l].astype(BF16), ln_mix_g[l][None, :], ln_mix_b[l][None, :],
                 B=B, S=S, tm=tm, alpha=alpha)
        h = _ffn(h, w_ffn_in[l][:, :D_FF].astype(BF16), w_ffn_in[l][:, D_FF:].astype(BF16),
                 conv_w[l], conv_b[l][None, :], w_ffn_out[l].astype(BF16),
                 ln_ffn_g[l][None, :], ln_ffn_b[l][None, :], B=B, S=S, tm=tm, alpha=alpha)
    return h.reshape(B, S, D)
```

```python
import functools

import jax
import jax.numpy as jnp
from jax import lax
from jax.experimental import pallas as pl
from jax.experimental.pallas import tpu as pltpu

D_MODEL = 1024
HEAD_DIM = 64
SWA_Q_HEADS = 8
SWA_KV_HEADS = 2
SWA_GROUP = SWA_Q_HEADS // SWA_KV_HEADS
SWA_WINDOW = 128
FOX_HEADS = 8
BLOCK = 128
D_FF = 2816
CONV_WIDTH = 3
LN_EPS = 1e-5
NEG_INF = -1e30
SWA_Q = SWA_Q_HEADS * HEAD_DIM
SWA_KV = SWA_KV_HEADS * HEAD_DIM
FOX_W = FOX_HEADS * HEAD_DIM

LANES = 128
F_ROWS = 16
AUG = 128
V_ROWS = HEAD_DIM + 16
VMEM_LIMIT = 56 * 1024 * 1024

ROW_TILE = 1024
SWA_TILE = 512
FOX_Q_TILE = 1024
FOX_K_TILE = 1024
FOX_WIDE = 2
FOX_GROUP = 2
FOX_DIAG_PIECE = 256
FOX_STEP_PIECE = 256
LN_PIECES = 4
MIX_PIECES = 4
LOG2E = 1.4426950408889634
FOX_BOUND_SLACK = 1.0 + 2.0 ** -6
FOX_BOUND_MARGIN = 2.0
FOX_MIN_DENOM = 2.0 ** -60

F32 = jnp.float32
BF16 = jnp.bfloat16


def _compiler_params(n_axes):
    return pltpu.CompilerParams(dimension_semantics=("arbitrary",) * n_axes,
                                vmem_limit_bytes=VMEM_LIMIT)


def _layer_norm(r, g, b):
    mu = jnp.mean(r, axis=-1, keepdims=True)
    d = r - mu
    var = jnp.mean(d * d, axis=-1, keepdims=True)
    return d * lax.rsqrt(var + LN_EPS) * g + b


def _inproj_kernel(x_ref, wtok_ref, btok_ref, wfm_ref, bfm_ref,
                   ka_ref, ga_ref, gb_ref, qat_ref, vat_ref, qbt_ref, kbt_ref, vbt_ref, ct_ref,
                   carry_ref, *, tm):
    i = pl.program_id(1)

    @pl.when(i == 0)
    def _():
        carry_ref[...] = jnp.zeros_like(carry_ref)

    xb = x_ref[...].astype(BF16)

    def fm_rows(lo, hi):
        z = lax.dot_general(wfm_ref[lo:hi, :], xb, (((1,), (1,)), ((), ())), preferred_element_type=F32)
        return z + jnp.concatenate([bfm_ref[lo:hi, :]] * (tm // LANES), axis=1)

    def tok_cols(lo, hi):
        return jnp.dot(xb, wtok_ref[:, lo:hi], preferred_element_type=F32) + btok_ref[:, lo:hi]

    o_qb = SWA_Q + SWA_KV
    o_kb, o_vb, o_f = o_qb + FOX_W, o_qb + 2 * FOX_W, o_qb + 3 * FOX_W
    z = fm_rows(o_vb, o_f + F_ROWS)
    vbt_ref[...] = z[:FOX_W].astype(BF16)

    f = z[FOX_W:FOX_W + FOX_HEADS]
    lf = jnp.minimum(f, 0.0) - jnp.log1p(jnp.exp(-jnp.abs(f)))
    lane = lax.broadcasted_iota(jnp.int32, lf.shape, 1)
    sh = 1
    while sh < tm:
        lf = lf + jnp.where(lane >= sh, pltpu.roll(lf, sh, 1), 0.0)
        sh *= 2

    c = lf + carry_ref[:, 0:1]
    ct_ref[...] = c * LOG2E
    carry_ref[...] = jnp.broadcast_to(c[:, tm - 1:tm], carry_ref.shape)

    z = fm_rows(0, o_qb)
    qat_ref[...] = (z[:SWA_Q] * LOG2E).astype(BF16)
    vat_ref[...] = z[SWA_Q:].astype(BF16)
    qbt_ref[...] = (fm_rows(o_qb, o_kb) * LOG2E).astype(BF16)
    kbt_ref[...] = fm_rows(o_kb, o_vb).astype(BF16)
    gb_ref[...] = tok_cols(SWA_KV + D_MODEL, SWA_KV + 2 * D_MODEL).astype(BF16)
    ga_ref[...] = tok_cols(SWA_KV, SWA_KV + D_MODEL).astype(BF16)
    ka_ref[...] = tok_cols(0, SWA_KV).astype(BF16)


def _inproj(h, wtok, btok, wfm, bfm, *, B, S, tm):
    T = B * S
    ns = S // tm
    n_tok = wtok.shape[1]
    n_fm = wfm.shape[0]
    row = lambda b, i: (b * ns + i, 0)
    fm = lambda b, i: (b, 0, i)
    const = lambda b, i: (0, 0)
    out_shape = (
        jax.ShapeDtypeStruct((T, SWA_KV), BF16),
        jax.ShapeDtypeStruct((T, D_MODEL), BF16),
        jax.ShapeDtypeStruct((T, D_MODEL), BF16),
        jax.ShapeDtypeStruct((B, SWA_Q, S), BF16),
        jax.ShapeDtypeStruct((B, SWA_KV, S), BF16),
        jax.ShapeDtypeStruct((B, FOX_W, S), BF16),
        jax.ShapeDtypeStruct((B, FOX_W, S), BF16),
        jax.ShapeDtypeStruct((B, FOX_W, S), BF16),
        jax.ShapeDtypeStruct((B, FOX_HEADS, S), F32),
    )
    out_specs = (
        pl.BlockSpec((tm, SWA_KV), row),
        pl.BlockSpec((tm, D_MODEL), row),
        pl.BlockSpec((tm, D_MODEL), row),
        pl.BlockSpec((None, SWA_Q, tm), fm),
        pl.BlockSpec((None, SWA_KV, tm), fm),
        pl.BlockSpec((None, FOX_W, tm), fm),
        pl.BlockSpec((None, FOX_W, tm), fm),
        pl.BlockSpec((None, FOX_W, tm), fm),
        pl.BlockSpec((None, FOX_HEADS, tm), fm),
    )
    in_specs = [
        pl.BlockSpec((tm, D_MODEL), row),
        pl.BlockSpec((D_MODEL, n_tok), const, pipeline_mode=pl.Buffered(1)),
        pl.BlockSpec((1, n_tok), const),
        pl.BlockSpec((n_fm, D_MODEL), const, pipeline_mode=pl.Buffered(1)),
        pl.BlockSpec((n_fm, LANES), const, pipeline_mode=pl.Buffered(1)),
    ]
    return pl.pallas_call(
        functools.partial(_inproj_kernel, tm=tm),
        grid=(B, ns),
        in_specs=in_specs,
        out_specs=out_specs,
        out_shape=out_shape,
        scratch_shapes=[pltpu.VMEM((FOX_HEADS, LANES), F32)],
        compiler_params=_compiler_params(2),
        name="inproj",
    )(h, wtok, btok, wfm, bfm)


def _swa_kernel(qat_ref, ka_ref, kap_ref, vat_ref, vap_ref, sink_ref, o_ref, bias_ref, *, ts):
    b = pl.program_id(0)
    i = pl.program_id(1)
    nblk = ts // BLOCK
    gw = SWA_GROUP * BLOCK

    @pl.when((b == 0) & (i == 0))
    def _():
        r = lax.broadcasted_iota(jnp.int32, (2 * BLOCK, BLOCK), 0)
        c = lax.broadcasted_iota(jnp.int32, (2 * BLOCK, BLOCK), 1)
        dist = BLOCK + c - r
        valid = (dist >= 0) & (dist < SWA_WINDOW)
        distf = dist.astype(F32)
        for g in range(SWA_KV_HEADS):
            for hh in range(SWA_GROUP):
                slope = 2.0 ** (-8.0 * (g * SWA_GROUP + hh + 1) / SWA_Q_HEADS)
                bias_ref[g, :, hh * BLOCK:(hh + 1) * BLOCK] = jnp.where(valid, (-slope * LOG2E) * distf, NEG_INF)

    kext = jnp.concatenate([kap_ref[...], ka_ref[...]], axis=0)
    vext = jnp.concatenate([vap_ref[...], vat_ref[...]], axis=1)
    zeros_q = jnp.zeros((HEAD_DIM, gw), BF16)
    pieces = [(n, g) for n in range(nblk) for g in range(SWA_KV_HEADS)]

    scores = []
    for n, g in pieces:
        qrow = jnp.concatenate(
            [qat_ref[(g * SWA_GROUP + hh) * HEAD_DIM:(g * SWA_GROUP + hh + 1) * HEAD_DIM,
                     n * BLOCK:(n + 1) * BLOCK] for hh in range(SWA_GROUP)], axis=1)
        q_pad = jnp.concatenate([qrow, zeros_q] if g == 0 else [zeros_q, qrow], axis=0)
        k_win = kext[n * BLOCK:(n + 2) * BLOCK]
        scores.append(jnp.dot(k_win, q_pad, preferred_element_type=F32))

    weights = []
    for (n, g), s in zip(pieces, scores):
        s = s + bias_ref[g]
        if n == 0:
            top = jnp.where(i == 0, NEG_INF, s[:BLOCK])
            s = jnp.concatenate([top, s[BLOCK:]], axis=0)
        sink = sink_ref[g] * LOG2E
        m = jnp.maximum(jnp.max(s, axis=0, keepdims=True), sink)
        p = jnp.exp2(s - m)
        denom = jnp.sum(p, axis=0, keepdims=True) + jnp.exp2(sink - m)
        weights.append((p.astype(BF16), denom))

    for (n, g), (p, denom) in zip(pieces, weights):
        v_win = vext[g * HEAD_DIM:(g + 1) * HEAD_DIM, n * BLOCK:(n + 2) * BLOCK]
        out = jnp.dot(v_win, p, preferred_element_type=F32) / denom
        for hh in range(SWA_GROUP):
            h0 = (g * SWA_GROUP + hh) * HEAD_DIM
            o_ref[h0:h0 + HEAD_DIM, n * BLOCK:(n + 1) * BLOCK] = (
                out[:, hh * BLOCK:(hh + 1) * BLOCK].astype(BF16))


def _swa(qat, ka, vat, sink_rows, *, B, S, ts):
    ns = S // ts
    nblk = ts // BLOCK
    ka3 = ka.reshape(B, S, SWA_KV)
    prev = lambda b, i: jnp.maximum(i * nblk - 1, 0)
    in_specs = [
        pl.BlockSpec((None, SWA_Q, ts), lambda b, i: (b, 0, i)),
        pl.BlockSpec((None, ts, SWA_KV), lambda b, i: (b, i, 0)),
        pl.BlockSpec((None, BLOCK, SWA_KV), lambda b, i: (b, prev(b, i), 0)),
        pl.BlockSpec((None, SWA_KV, ts), lambda b, i: (b, 0, i)),
        pl.BlockSpec((None, SWA_KV, BLOCK), lambda b, i: (b, 0, prev(b, i))),
        pl.BlockSpec((SWA_KV_HEADS, 1, SWA_GROUP * BLOCK), lambda b, i: (0, 0, 0)),
    ]
    return pl.pallas_call(
        functools.partial(_swa_kernel, ts=ts),
        grid=(B, ns),
        in_specs=in_specs,
        out_specs=pl.BlockSpec((None, SWA_Q, ts), lambda b, i: (b, 0, i)),
        out_shape=jax.ShapeDtypeStruct((B, SWA_Q, S), BF16),
        scratch_shapes=[pltpu.VMEM((SWA_KV_HEADS, 2 * BLOCK, SWA_GROUP * BLOCK), F32)],
        compiler_params=_compiler_params(2),
        name="swa",
    )(qat, ka3, ka3, vat, vat, sink_rows)


def _split3(c):
    hi = c.astype(BF16).astype(F32)
    r = c - hi
    mid = r.astype(BF16).astype(F32)
    lo = (r - mid).astype(BF16).astype(F32)
    return hi, mid, lo


def _rows8(a, b, c, n):
    rid = lax.broadcasted_iota(jnp.int32, (8, n), 0)
    return jnp.where(rid == 0, a, jnp.where(rid == 1, b, jnp.where(rid == 2, c, 0.0)))


def _fox_kernel(qt_ref, kt_ref, vt_ref, c_ref, cq_ref, o_ref, kaug_ref, vaug_ref, knorm_ref, acc_ref,
                p_ref, *, S, tq, tk):
    i = pl.program_id(2)
    nkt = S // tk
    pad_rows = AUG - HEAD_DIM - 16
    heads = range(FOX_GROUP)
    hrows = [slice(hd * HEAD_DIM, (hd + 1) * HEAD_DIM) for hd in heads]

    @pl.when(i == 0)
    def _():
        for hd in heads:
            ksq = jnp.zeros((1, tk), F32)
            for t in range(nkt):
                sl = slice(t * tk, (t + 1) * tk)
                hi, mid, lo = _split3(c_ref[hd:hd + 1, sl])
                kf = kt_ref[hrows[hd], sl].astype(F32)
                ksq = jnp.maximum(ksq, jnp.sum(kf * kf, axis=0, keepdims=True))
                x = jnp.concatenate([kf, _rows8(1.0, 1.0, 1.0, tk),
                                     _rows8(-hi, -mid, -lo, tk), jnp.zeros((pad_rows, tk), F32)], axis=0)
                kaug_ref[hd, t] = x.T.astype(BF16)
                rid = lax.broadcasted_iota(jnp.int32, (V_ROWS - HEAD_DIM, tk), 0)
                vaug_ref[hd, t] = jnp.concatenate(
                    [vt_ref[hrows[hd], sl], jnp.where(rid == 0, 1.0, 0.0).astype(BF16)], axis=0)
            knorm_ref[hd] = jnp.broadcast_to(jnp.sqrt(jnp.max(ksq, axis=1, keepdims=True)), knorm_ref.shape[1:])

    q_aug = []
    for hd in heads:
        qf = qt_ref[hrows[hd], :].astype(F32)
        qnorm = jnp.sqrt(jnp.sum(qf * qf, axis=0, keepdims=True))
        bound = qnorm * knorm_ref[hd, 0:1, 0:1] * FOX_BOUND_SLACK + FOX_BOUND_MARGIN
        hi, mid, lo = _split3(cq_ref[hd:hd + 1, :] - bound)
        q_aug.append(jnp.concatenate([qf, _rows8(hi, mid, lo, tq), _rows8(1.0, 1.0, 1.0, tq),
                                      jnp.zeros((pad_rows, tq), F32)], axis=0).astype(BF16))
    nfull = (i * tq) // tk

    wrows = FOX_WIDE * tk

    def wide_first(t):
        for hd in heads:
            k = kaug_ref[hd, pl.ds(t * FOX_WIDE, FOX_WIDE)].reshape(wrows, AUG)
            p_ref[hd] = jnp.exp2(jnp.dot(k, q_aug[hd], preferred_element_type=F32)).astype(BF16)

    def wide_step(t, accs):
        accs = list(accs)
        for hd in heads:
            for r0 in range(0, wrows, FOX_STEP_PIECE):
                w, off = divmod(r0, tk)
                accs[hd] = accs[hd] + jnp.dot(vaug_ref[hd, (t - 1) * FOX_WIDE + w, :, off:off + FOX_STEP_PIECE],
                                              p_ref[hd, r0:r0 + FOX_STEP_PIECE, :], preferred_element_type=F32)
                s = jnp.dot(kaug_ref[hd, t * FOX_WIDE + w, off:off + FOX_STEP_PIECE, :], q_aug[hd],
                            preferred_element_type=F32)
                p_ref[hd, r0:r0 + FOX_STEP_PIECE, :] = jnp.exp2(s).astype(BF16)
        return tuple(accs)

    nwide = nfull // FOX_WIDE

    @pl.when(nwide >= 1)
    def _():
        wide_first(0)

    accs = lax.fori_loop(1, nwide, wide_step, tuple(jnp.zeros((V_ROWS, tq), F32) for _ in heads))

    def finish(accs, odd, pending):
        starts = range(0, tk, FOX_DIAG_PIECE)
        full = [jnp.dot(kaug_ref[hd, nfull - 1], q_aug[hd], preferred_element_type=F32) if odd else None
                for hd in heads]
        diag = [[jnp.dot(kaug_ref[hd, nfull, c0:c0 + FOX_DIAG_PIECE, :], q_aug[hd][:, c0:],
                         preferred_element_type=F32) for c0 in starts] for hd in heads]
        accs = list(accs)
        if pending:
            for hd in heads:
                for w in range(FOX_WIDE):
                    accs[hd] = accs[hd] + jnp.dot(vaug_ref[hd, (nwide - 1) * FOX_WIDE + w],
                                                  p_ref[hd, w * tk:(w + 1) * tk, :], preferred_element_type=F32)
        if odd:
            for hd in heads:
                accs[hd] = accs[hd] + jnp.dot(vaug_ref[hd, nfull - 1], jnp.exp2(full[hd]).astype(BF16),
                                              preferred_element_type=F32)
        weights = []
        for hd in heads:
            w = []
            for s in diag[hd]:
                kidx = lax.broadcasted_iota(jnp.int32, s.shape, 0)
                qidx = lax.broadcasted_iota(jnp.int32, s.shape, 1)
                w.append(jnp.exp2(jnp.where(kidx <= qidx, s, NEG_INF)).astype(BF16))
            weights.append(w)
        for hd in heads:
            acc = accs[hd]
            for c0, p in zip(starts, weights[hd]):
                pv = jnp.dot(vaug_ref[hd, nfull, :, c0:c0 + FOX_DIAG_PIECE], p, preferred_element_type=F32)
                acc = jnp.concatenate([acc[:, :c0], acc[:, c0:] + pv], axis=1) if c0 else acc + pv
            acc_ref[hd] = acc

    for odd in (False, True):
        for pending in (False, True):
            @pl.when((nfull % FOX_WIDE == int(odd)) & ((nwide >= 1) == pending))
            def _(odd=odd, pending=pending):
                finish(accs, odd, pending)

    dmin = None
    for hd in heads:
        acc = acc_ref[hd]
        denom = acc[HEAD_DIM:HEAD_DIM + 1]
        o_ref[hrows[hd], :] = (acc[:HEAD_DIM] / denom).astype(BF16)
        dmin = denom if dmin is None else jnp.minimum(dmin, denom)

    @pl.when(jnp.logical_not(jnp.min(dmin) >= FOX_MIN_DENOM))
    def _():
        for hd in heads:
            def tile_exact(j, carry, masked):
                m, acc = carry
                s = jnp.dot(kaug_ref[hd, j], q_aug[hd], preferred_element_type=F32)
                if masked:
                    kidx = j * tk + lax.broadcasted_iota(jnp.int32, s.shape, 0)
                    qidx = i * tq + lax.broadcasted_iota(jnp.int32, s.shape, 1)
                    s = jnp.where(kidx <= qidx, s, NEG_INF)
                m_new = jnp.maximum(m, jnp.max(s, axis=0, keepdims=True))
                p = jnp.exp2(s - m_new).astype(BF16)
                acc = acc * jnp.exp2(m - m_new) + jnp.dot(vaug_ref[hd, j], p, preferred_element_type=F32)
                return m_new, acc

            carry = (jnp.full((1, tq), NEG_INF, F32), jnp.zeros((V_ROWS, tq), F32))
            carry = lax.fori_loop(0, nfull, lambda j, cr: tile_exact(j, cr, False), carry)
            _, acc2 = tile_exact(nfull, carry, True)
            o_ref[hrows[hd], :] = (acc2[:HEAD_DIM] / acc2[HEAD_DIM:HEAD_DIM + 1]).astype(BF16)


def _fox(qbt, kbt, vbt, ct, *, B, S, tq, tk):
    nq = S // tq
    ngrp = FOX_HEADS // FOX_GROUP
    gw = FOX_GROUP * HEAD_DIM
    c3 = ct.reshape(B * ngrp, FOX_GROUP, S)
    in_specs = [
        pl.BlockSpec((None, gw, tq), lambda b, h, i: (b, h, i)),
        pl.BlockSpec((None, gw, S), lambda b, h, i: (b, h, 0)),
        pl.BlockSpec((None, gw, S), lambda b, h, i: (b, h, 0)),
        pl.BlockSpec((None, FOX_GROUP, S), lambda b, h, i: (b * ngrp + h, 0, 0)),
        pl.BlockSpec((None, FOX_GROUP, tq), lambda b, h, i: (b * ngrp + h, 0, i)),
    ]
    return pl.pallas_call(
        functools.partial(_fox_kernel, S=S, tq=tq, tk=tk),
        grid=(B, ngrp, nq),
        in_specs=in_specs,
        out_specs=pl.BlockSpec((None, gw, tq), lambda b, h, i: (b, h, i)),
        out_shape=jax.ShapeDtypeStruct((B, FOX_W, S), BF16),
        scratch_shapes=[pltpu.VMEM((FOX_GROUP, S // tk, tk, AUG), BF16),
                        pltpu.VMEM((FOX_GROUP, S // tk, V_ROWS, tk), BF16),
                        pltpu.VMEM((FOX_GROUP, 8, LANES), F32),
                        pltpu.VMEM((FOX_GROUP, V_ROWS, tq), F32),
                        pltpu.VMEM((FOX_GROUP, FOX_WIDE * tk, tq), BF16)],
        compiler_params=_compiler_params(3),
        name="fox",
    )(qbt, kbt, vbt, c3, c3)


def _mix_kernel(h_ref, yat_ref, ybt_ref, ga_ref, gb_ref, wpa_ref, wpb_ref, wout_ref, lng_ref, lnb_ref,
                o_ref, *, alpha):
    tn = (((0,), (0,)), ((), ()))
    rows = h_ref.shape[0] // MIX_PIECES
    pieces = [slice(r * rows, (r + 1) * rows) for r in range(MIX_PIECES)]
    proj = [(lax.dot_general(yat_ref[:, rs], wpa_ref[...], tn, preferred_element_type=F32),
             lax.dot_general(ybt_ref[:, rs], wpb_ref[...], tn, preferred_element_type=F32)) for rs in pieces]
    for rs, (ya, yb) in zip(pieces, proj):
        merged = (jax.nn.sigmoid(ga_ref[rs, :].astype(F32)) * ya
                  + jax.nn.sigmoid(gb_ref[rs, :].astype(F32)) * yb).astype(BF16)
        mix = jnp.dot(merged, wout_ref[...], preferred_element_type=F32)
        o_ref[rs, :] = _layer_norm(alpha * h_ref[rs, :] + mix, lng_ref[...], lnb_ref[...])


def _mix(h, yat, ybt, ga, gb, wpa, wpb, wout, lng, lnb, *, B, S, tm, alpha):
    T = B * S
    ns = S // tm
    row = lambda b, i: (b * ns + i, 0)
    fm = lambda b, i: (b, 0, i)
    const = lambda b, i: (0, 0)
    in_specs = [
        pl.BlockSpec((tm, D_MODEL), row),
        pl.BlockSpec((None, SWA_Q, tm), fm),
        pl.BlockSpec((None, FOX_W, tm), fm),
        pl.BlockSpec((tm, D_MODEL), row),
        pl.BlockSpec((tm, D_MODEL), row),
        pl.BlockSpec((SWA_Q, D_MODEL), const),
        pl.BlockSpec((FOX_W, D_MODEL), const),
        pl.BlockSpec((D_MODEL, D_MODEL), const),
        pl.BlockSpec((1, D_MODEL), const),
        pl.BlockSpec((1, D_MODEL), const),
    ]
    return pl.pallas_call(
        functools.partial(_mix_kernel, alpha=alpha),
        grid=(B, ns),
        in_specs=in_specs,
        out_specs=pl.BlockSpec((tm, D_MODEL), row),
        out_shape=jax.ShapeDtypeStruct((T, D_MODEL), F32),
        compiler_params=_compiler_params(2),
        name="mix",
    )(h, yat, ybt, ga, gb, wpa, wpb, wout, lng, lnb)


FFN_HALO = 16
FFN_CHUNK = 256


def _ffn_kernel(h_ref, hp_ref, wg_ref, wu_ref, cw_ref, cb_ref, wo_ref, lng_ref, lnb_ref, o_ref, a_ref,
                *, tm, alpha):
    i = pl.program_id(1)
    h = h_ref[...]
    hb = h.astype(BF16)
    hp = jnp.where(i == 0, 0.0, hp_ref[...]).astype(BF16)
    hx = jnp.concatenate([hp, hb], axis=0)
    for c in range(D_FF // FFN_CHUNK):
        sl = slice(c * FFN_CHUNK, (c + 1) * FFN_CHUNK)
        g = jnp.dot(hx, wg_ref[:, sl], preferred_element_type=F32)
        u = jnp.dot(hb, wu_ref[:, sl], preferred_element_type=F32)
        conv = cb_ref[:, sl] + cw_ref[2:3, sl] * g[FFN_HALO:]
        conv = conv + cw_ref[1:2, sl] * pltpu.roll(g, 1, 0)[FFN_HALO:]
        conv = conv + cw_ref[0:1, sl] * pltpu.roll(g, 2, 0)[FFN_HALO:]
        a_ref[:, sl] = (conv * jax.nn.sigmoid(conv) * u).astype(BF16)
    rows = tm // LN_PIECES
    for r in range(LN_PIECES):
        rs = slice(r * rows, (r + 1) * rows)
        ffn = jnp.dot(a_ref[rs, :], wo_ref[...], preferred_element_type=F32)
        o_ref[rs, :] = _layer_norm(alpha * h[rs] + ffn, lng_ref[...], lnb_ref[...])


def _ffn(h, wg, wu, cw, cb, wo, lng, lnb, *, B, S, tm, alpha):
    T = B * S
    ns = S // tm
    row = lambda b, i: (b * ns + i, 0)
    halo = lambda b, i: (jnp.maximum((b * ns + i) * (tm // FFN_HALO) - 1, 0), 0)
    const = lambda b, i: (0, 0)
    in_specs = [
        pl.BlockSpec((tm, D_MODEL), row),
        pl.BlockSpec((FFN_HALO, D_MODEL), halo),
        pl.BlockSpec((D_MODEL, D_FF), const, pipeline_mode=pl.Buffered(1)),
        pl.BlockSpec((D_MODEL, D_FF), const, pipeline_mode=pl.Buffered(1)),
        pl.BlockSpec((CONV_WIDTH, D_FF), const),
        pl.BlockSpec((1, D_FF), const),
        pl.BlockSpec((D_FF, D_MODEL), const, pipeline_mode=pl.Buffered(1)),
        pl.BlockSpec((1, D_MODEL), const),
        pl.BlockSpec((1, D_MODEL), const),
    ]
    return pl.pallas_call(
        functools.partial(_ffn_kernel, tm=tm, alpha=alpha),
        grid=(B, ns),
        in_specs=in_specs,
        out_specs=pl.BlockSpec((tm, D_MODEL), row),
        out_shape=jax.ShapeDtypeStruct((T, D_MODEL), F32),
        scratch_shapes=[pltpu.VMEM((tm, D_FF), BF16)],
        compiler_params=_compiler_params(2),
        name="ffn",
    )(h, h, wg, wu, cw, cb, wo, lng, lnb)


def _prep_inproj_weights(w_in, b_in):
    scale = HEAD_DIM ** -0.5
    o = 0
    parts = {}
    for name, n in (("qa", SWA_Q), ("ka", SWA_KV), ("va", SWA_KV), ("qb", FOX_W), ("kb", FOX_W),
                    ("vb", FOX_W), ("f", FOX_HEADS), ("ga", D_MODEL), ("gb", D_MODEL)):
        parts[name] = (w_in[:, o:o + n], b_in[o:o + n])
        o += n
    tok = [parts["ka"], parts["ga"], parts["gb"]]
    wtok = jnp.concatenate([w for w, _ in tok], axis=1).astype(BF16)
    btok = jnp.concatenate([b for _, b in tok])[None, :]
    pad_w = jnp.zeros((D_MODEL, F_ROWS - FOX_HEADS), F32)
    pad_b = jnp.zeros((F_ROWS - FOX_HEADS,), F32)
    fm_w = [parts["qa"][0] * scale, parts["va"][0], parts["qb"][0] * scale, parts["kb"][0],
            parts["vb"][0], parts["f"][0], pad_w]
    fm_b = [parts["qa"][1] * scale, parts["va"][1], parts["qb"][1] * scale, parts["kb"][1],
            parts["vb"][1], parts["f"][1], pad_b]
    wfm = jnp.concatenate(fm_w, axis=1).T.astype(BF16)
    bfm = jnp.broadcast_to(jnp.concatenate(fm_b)[:, None], (wfm.shape[0], LANES))
    return wtok, btok, wfm, bfm


def kernel(x, ln_mix_g, ln_mix_b, w_in, b_in, attn_sinks, w_proj_a, w_proj_b, w_out, ln_ffn_g, ln_ffn_b,
           w_ffn_in, conv_w, conv_b, w_ffn_out):
    B, S, D = x.shape
    assert D == D_MODEL and S % BLOCK == 0
    depth = w_in.shape[0]
    alpha = (2 * depth) ** 0.25
    tm = min(ROW_TILE, S)
    ts = min(SWA_TILE, S)
    tq = min(FOX_Q_TILE, S)
    tk = min(FOX_K_TILE, S)
    assert S % tm == 0 and S % ts == 0 and S % tq == 0 and tq == tk
    assert FOX_WIDE == 2 and FOX_HEADS % FOX_GROUP == 0 and tm % LN_PIECES == 0 and tm % MIX_PIECES == 0

    h = x.reshape(B * S, D)
    for l in range(depth):
        wtok, btok, wfm, bfm = _prep_inproj_weights(w_in[l], b_in[l])
        ka, ga, gb, qat, vat, qbt, kbt, vbt, ct = _inproj(h, wtok, btok, wfm, bfm, B=B, S=S, tm=tm)
        sink_rows = jnp.broadcast_to(
            attn_sinks[l].reshape(SWA_KV_HEADS, SWA_GROUP, 1), (SWA_KV_HEADS, SWA_GROUP, BLOCK)
        ).reshape(SWA_KV_HEADS, 1, SWA_GROUP * BLOCK)
        yat = _swa(qat, ka, vat, sink_rows, B=B, S=S, ts=ts)
        ybt = _fox(qbt, kbt, vbt, ct, B=B, S=S, tq=tq, tk=tk)
        h = _mix(h, yat, ybt, ga, gb, w_proj_a[l].astype(BF16), w_proj_b[l].astype(BF16),
                 w_out[l].astype(BF16), ln_mix_g[l][None, :], ln_mix_b[l][None, :],
                 B=B, S=S, tm=tm, alpha=alpha)
        h = _ffn(h, w_ffn_in[l][:, :D_FF].astype(BF16), w_ffn_in[l][:, D_FF:].astype(BF16),
                 conv_w[l], conv_b[l][None, :], w_ffn_out[l].astype(BF16),
                 ln_ffn_g[l][None, :], ln_ffn_b[l][None, :], B=B, S=S, tm=tm, alpha=alpha)
    return h.reshape(B, S, D)
```
